```python
import math
import jax, jax.numpy as jnp
from jax import lax
import numpy as np

D_MODEL = 2048
BATCH = 2
SEQ = 8192
DEPTH = 4

N_MIXERS = 2
N_SSD_LAYERS = (DEPTH + 1) // 2
N_RET_LAYERS = DEPTH // 2
CHUNK = 128
NORM_EPS = 1e-6

SSD_EXPAND = 2
SSD_D_INNER = SSD_EXPAND * D_MODEL
SSD_HEAD_DIM = 64
SSD_HEADS = SSD_D_INNER // SSD_HEAD_DIM
SSD_GROUPS = 8
SSD_HEADS_PER_GROUP = SSD_HEADS // SSD_GROUPS
SSD_STATE = 128
SSD_CONV = 4
SSD_CONV_DIM = SSD_D_INNER + 2 * SSD_GROUPS * SSD_STATE
SSD_IN = 2 * SSD_D_INNER + 2 * SSD_GROUPS * SSD_STATE + SSD_HEADS
SSD_NORM_GROUP = SSD_D_INNER // SSD_GROUPS

RET_HEADS = 8
RET_QK_DIM = D_MODEL
RET_V_DIM = 2 * D_MODEL
RET_HEAD_QK = RET_QK_DIM // RET_HEADS
RET_HEAD_V = RET_V_DIM // RET_HEADS
RET_IN = 2 * RET_QK_DIM + 2 * RET_V_DIM
ROPE_BASE = 10000.0

MEM_TOKENS = 256
MEM_HEADS = 4
MEM_HEAD_DIM = D_MODEL // MEM_HEADS

FFN_HIDDEN = ((8 * D_MODEL // 3 + 255) // 256) * 256

kernel_name = 'ssd_retention_interleaved_memory_trunk'


def rms_norm(x, w):
    xf = x.astype(jnp.float32)
    y = xf * lax.rsqrt(jnp.mean(xf * xf, axis=-1, keepdims=True) + NORM_EPS)
    return (y * w.astype(jnp.float32)).astype(x.dtype)


def causal_depthwise_conv(u, w, b):
    out = lax.conv_general_dilated(
        u, w[:, None, :].astype(u.dtype), window_strides=(1,), padding=[(SSD_CONV - 1, 0)],
        dimension_numbers=('NWC', 'WIO', 'NWC'), feature_group_count=u.shape[-1])
    return out + b.astype(u.dtype)


def ssd_mixer(u, w_in, conv_w, conv_b, dt_bias, a_log, d_skip, norm_w, w_out):
    f32 = jnp.float32
    bsz, seq, _ = u.shape
    nc = seq // CHUNK
    G, E, P, N = SSD_GROUPS, SSD_HEADS_PER_GROUP, SSD_HEAD_DIM, SSD_STATE
    zxbcdt = u @ w_in
    z = zxbcdt[..., :SSD_D_INNER]
    xbc = zxbcdt[..., SSD_D_INNER:SSD_D_INNER + SSD_CONV_DIM]
    dt_raw = zxbcdt[..., SSD_D_INNER + SSD_CONV_DIM:]
    xbc = jax.nn.silu(causal_depthwise_conv(xbc, conv_w, conv_b)).astype(f32)
    xs = xbc[..., :SSD_D_INNER].reshape(bsz, nc, CHUNK, G, E, P)
    b_c = xbc[..., SSD_D_INNER:SSD_D_INNER + G * N].reshape(bsz, nc, CHUNK, G, N)
    c_c = xbc[..., SSD_D_INNER + G * N:].reshape(bsz, nc, CHUNK, G, N)
    dt = jax.nn.softplus(dt_raw.astype(f32) + dt_bias.astype(f32)).reshape(bsz, nc, CHUNK, G, E)
    a = -jnp.exp(a_log.astype(f32)).reshape(G, E)
    a_cum = jnp.cumsum(dt * a, axis=2)
    xdt = xs * dt[..., None]
    a_t = jnp.moveaxis(a_cum, 2, -1)
    causal = jnp.tril(jnp.ones((CHUNK, CHUNK), dtype=bool))
    decay = jnp.exp(jnp.where(causal, a_t[..., :, None] - a_t[..., None, :], -jnp.inf))
    cb = jnp.einsum('bclgn,bcsgn->bcgls', c_c, b_c)
    y_diag = jnp.einsum('bcgels,bcsgep->bclgep', cb[:, :, :, None] * decay, xdt)
    states = jnp.einsum('bclgn,bclgep->bcgepn', b_c,
                        xdt * jnp.exp(a_cum[:, :, -1:] - a_cum)[..., None])
    chunk_decay = jnp.exp(a_cum[:, :, -1])

    def step(h, inp):
        st, dec = inp
        return h * dec[..., None, None] + st, h

    h0 = jnp.zeros((bsz, G, E, P, N), f32)
    _, prev = lax.scan(step, h0, (jnp.moveaxis(states, 1, 0), jnp.moveaxis(chunk_decay, 1, 0)))
    prev = jnp.moveaxis(prev, 0, 1)
    y_off = jnp.einsum('bclgn,bcgepn->bclgep', c_c, prev) * jnp.exp(a_cum)[..., None]
    y = y_diag + y_off + xs * d_skip.astype(f32).reshape(G, E, 1)
    y = y.reshape(bsz, seq, SSD_D_INNER) * jax.nn.silu(z.astype(f32))
    yg = y.reshape(bsz, seq, G, SSD_NORM_GROUP)
    yg = yg * lax.rsqrt(jnp.mean(yg * yg, axis=-1, keepdims=True) + NORM_EPS)
    y = yg.reshape(bsz, seq, SSD_D_INNER) * norm_w.astype(f32)
    return y.astype(u.dtype) @ w_out


def rotate_pairs(x, cos, sin):
    xr = x.reshape(*x.shape[:-1], -1, 2)
    x1, x2 = xr[..., 0], xr[..., 1]
    return jnp.stack([x1 * cos - x2 * sin, x2 * cos + x1 * sin], axis=-1).reshape(x.shape)


def retention_mixer(u, w_in, w_out):
    f32 = jnp.float32
    bsz, seq, _ = u.shape
    nc = seq // CHUNK
    H, DK, DV = RET_HEADS, RET_HEAD_QK, RET_HEAD_V
    proj = u @ w_in
    q = proj[..., :RET_QK_DIM].astype(f32).reshape(bsz, seq, H, DK)
    k = proj[..., RET_QK_DIM:2 * RET_QK_DIM].astype(f32).reshape(bsz, seq, H, DK)
    v = proj[..., 2 * RET_QK_DIM:2 * RET_QK_DIM + RET_V_DIM].astype(f32).reshape(bsz, seq, H, DV)
    gate = proj[..., 2 * RET_QK_DIM + RET_V_DIM:]
    pos = jnp.arange(seq, dtype=f32)
    inv_freq = 1.0 / (ROPE_BASE ** jnp.linspace(0.0, 1.0, DK // 2, dtype=f32))
    theta = pos[:, None] * inv_freq[None, :]
    cos = jnp.cos(theta)[:, None, :]
    sin = jnp.sin(theta)[:, None, :]
    q = rotate_pairs(q, cos, sin)
    k = rotate_pairs(k, cos, sin) * DK ** -0.5
    log_gamma = jnp.log1p(-jnp.exp2(-5.0 - jnp.arange(H, dtype=f32)))
    q = q.reshape(bsz, nc, CHUNK, H, DK)
    k = k.reshape(bsz, nc, CHUNK, H, DK)
    v = v.reshape(bsz, nc, CHUNK, H, DV)
    idx = jnp.arange(CHUNK, dtype=f32)
    causal = jnp.tril(jnp.ones((CHUNK, CHUNK), dtype=bool))
    intra_decay = jnp.exp(jnp.where(causal[None],
                                    (idx[:, None] - idx[None, :])[None] * log_gamma[:, None, None],
                                    -jnp.inf))
    scores = jnp.einsum('bclhd,bcshd->bchls', q, k) * intra_decay
    intra = jnp.einsum('bchls,bcshv->bclhv', scores, v)
    q_decay = jnp.exp((idx + 1.0)[:, None] * log_gamma)[..., None]
    k_decay = jnp.exp((CHUNK - 1.0 - idx)[:, None] * log_gamma)[..., None]
    chunk_decay = jnp.exp(CHUNK * log_gamma)[:, None, None]

    def step(state, inp):
        qc, kc, vc = inp
        cross = jnp.einsum('blhd,bhdv->blhv', qc * q_decay, state)
        state = state * chunk_decay + jnp.einsum('bshd,bshv->bhdv', kc * k_decay, vc)
        return state, cross

    s0 = jnp.zeros((bsz, H, DK, DV), f32)
    _, cross = lax.scan(step, s0, (jnp.moveaxis(q, 1, 0), jnp.moveaxis(k, 1, 0), jnp.moveaxis(v, 1, 0)))
    y = (intra + jnp.moveaxis(cross, 0, 1)).reshape(bsz, seq, H, DV)
    y = y * lax.rsqrt(jnp.mean(y * y, axis=-1, keepdims=True) + NORM_EPS)
    y = y.reshape(bsz, seq, RET_V_DIM) * jax.nn.silu(gate.astype(f32))
    return y.astype(u.dtype) @ w_out


def memory_cross_attention(u, mem_n, w_q, w_kv, w_o):
    bsz, seq, _ = u.shape
    q = (u @ w_q).reshape(bsz, seq, MEM_HEADS, MEM_HEAD_DIM)
    kv = mem_n @ w_kv
    k = kv[..., :D_MODEL].reshape(bsz, -1, MEM_HEADS, MEM_HEAD_DIM)
    v = kv[..., D_MODEL:].reshape(bsz, -1, MEM_HEADS, MEM_HEAD_DIM)
    scores = jnp.einsum('bshd,bmhd->bhsm', q, k).astype(jnp.float32) * MEM_HEAD_DIM ** -0.5
    probs = jax.nn.softmax(scores, axis=-1).astype(v.dtype)
    o = jnp.einsum('bhsm,bmhd->bshd', probs, v).reshape(bsz, seq, D_MODEL)
    return o @ w_o


def swiglu(u, w_in, w_out):
    gu = u @ w_in
    return (jax.nn.silu(gu[..., :FFN_HIDDEN]) * gu[..., FFN_HIDDEN:]) @ w_out


def setup_inputs(seed: int = 0) -> dict:
    key = jax.random.key(seed)
    ks = jax.random.split(key, 24)
    f32 = jnp.float32

    def dense(k, shape, fan_in):
        return jax.random.normal(k, shape, f32) * fan_in ** -0.5

    def gain(k, shape):
        return 1.0 + 0.02 * jax.random.normal(k, shape, f32)

    dt0 = jnp.exp(jax.random.uniform(ks[5], (N_SSD_LAYERS, SSD_HEADS), f32)
                  * (math.log(0.1) - math.log(0.001)) + math.log(0.001))
    return {
        'x': jax.random.normal(ks[0], (BATCH, SEQ, D_MODEL), f32),
        'mem': jax.random.normal(ks[1], (BATCH, MEM_TOKENS, D_MODEL), f32),
        'ssd_in_w': dense(ks[2], (N_SSD_LAYERS, D_MODEL, SSD_IN), D_MODEL),
        'ssd_conv_w': dense(ks[3], (N_SSD_LAYERS, SSD_CONV, SSD_CONV_DIM), SSD_CONV),
        'ssd_conv_b': 0.01 * jax.random.normal(ks[4], (N_SSD_LAYERS, SSD_CONV_DIM), f32),
        'ssd_dt_bias': dt0 + jnp.log(-jnp.expm1(-dt0)),
        'ssd_a_log': jnp.log(jax.random.uniform(ks[6], (N_SSD_LAYERS, SSD_HEADS), f32, minval=1.0, maxval=16.0)),
        'ssd_d': 1.0 + 0.1 * jax.random.normal(ks[7], (N_SSD_LAYERS, SSD_HEADS), f32),
        'ssd_norm_w': gain(ks[8], (N_SSD_LAYERS, SSD_D_INNER)),
        'ssd_out_w': dense(ks[9], (N_SSD_LAYERS, SSD_D_INNER, D_MODEL), SSD_D_INNER),
        'ret_in_w': dense(ks[10], (N_RET_LAYERS, D_MODEL, RET_IN), D_MODEL),
        'ret_out_w': dense(ks[11], (N_RET_LAYERS, RET_V_DIM, D_MODEL), RET_V_DIM),
        'mixer_norm_w': gain(ks[12], (DEPTH, D_MODEL)),
        'mem_q_norm_w': gain(ks[13], (DEPTH, D_MODEL)),
        'mem_kv_norm_w': gain(ks[14], (DEPTH, D_MODEL)),
        'mem_q_w': dense(ks[15], (DEPTH, D_MODEL, D_MODEL), D_MODEL),
        'mem_kv_w': dense(ks[16], (DEPTH, D_MODEL, 2 * D_MODEL), D_MODEL),
        'mem_o_w': dense(ks[17], (DEPTH, D_MODEL, D_MODEL), D_MODEL),
        'ffn_norm_w': gain(ks[18], (DEPTH, D_MODEL)),
        'ffn_in_w': dense(ks[19], (DEPTH, D_MODEL, 2 * FFN_HIDDEN), D_MODEL),
        'ffn_out_w': dense(ks[20], (DEPTH, FFN_HIDDEN, D_MODEL), FFN_HIDDEN),
        'final_norm_w': gain(ks[21], (D_MODEL,)),
    }


def reference(x, mem, ssd_in_w, ssd_conv_w, ssd_conv_b, ssd_dt_bias, ssd_a_log, ssd_d, ssd_norm_w,
              ssd_out_w, ret_in_w, ret_out_w, mixer_norm_w, mem_q_norm_w, mem_kv_norm_w, mem_q_w,
              mem_kv_w, mem_o_w, ffn_norm_w, ffn_in_w, ffn_out_w, final_norm_w):
    for i in range(DEPTH):
        j = i // N_MIXERS
        h = rms_norm(x, mixer_norm_w[i])
        if i % N_MIXERS == 0:
            y = ssd_mixer(h, ssd_in_w[j], ssd_conv_w[j], ssd_conv_b[j], ssd_dt_bias[j], ssd_a_log[j],
                          ssd_d[j], ssd_norm_w[j], ssd_out_w[j])
        else:
            y = retention_mixer(h, ret_in_w[j], ret_out_w[j])
        x = x + y.astype(x.dtype)
        h = rms_norm(x, mem_q_norm_w[i])
        mem_n = rms_norm(mem, mem_kv_norm_w[i])
        x = x + memory_cross_attention(h, mem_n, mem_q_w[i], mem_kv_w[i], mem_o_w[i]).astype(x.dtype)
        h = rms_norm(x, ffn_norm_w[i])
        x = x + swiglu(h, ffn_in_w[i], ffn_out_w[i]).astype(x.dtype)
    return rms_norm(x, final_norm_w)
```

```python
import functools
import math

import jax
import jax.numpy as jnp
from jax import lax
from jax.experimental import pallas as pl
from jax.experimental.pallas import tpu as pltpu

NORM_EPS = 1e-6
CHUNK = 128
LANES = 128
SUBLANES = 8
VMEM_LIMIT_BYTES = 56 * 1024 * 1024

SSD_HEAD_DIM = 64
SSD_GROUPS = 8
SSD_STATE = 128
SSD_CONV = 4
RET_HEADS = 8
MEM_HEADS = 4
ROPE_BASE = 10000.0

F32 = jnp.float32
BF16 = jnp.bfloat16


def _params(semantics):
    return pltpu.CompilerParams(dimension_semantics=semantics,
                                vmem_limit_bytes=VMEM_LIMIT_BYTES)


def _silu(v):
    return v * jax.nn.sigmoid(v)


def _rms_rows(x, w):
    ms = jnp.mean(x * x, axis=-1, keepdims=True)
    return x * lax.rsqrt(ms + NORM_EPS) * w


def _norm_mm_kernel(x_ref, nw_ref, w_ref, o_ref, h_ref):
    @pl.when(pl.program_id(1) == 0)
    def _():
        h_ref[...] = _rms_rows(x_ref[...], nw_ref[...]).astype(h_ref.dtype)

    o_ref[...] = jnp.dot(h_ref[...], w_ref[...],
                         preferred_element_type=F32).astype(o_ref.dtype)


def norm_matmul(x, nw, w, *, tm, tn, out_dtype, name):
    m, k = x.shape
    n = w.shape[1]
    tm = min(tm, m)
    tn = min(tn, n)
    return pl.pallas_call(
        _norm_mm_kernel,
        grid=(m // tm, n // tn),
        in_specs=[pl.BlockSpec((tm, k), lambda i, j: (i, 0)),
                  pl.BlockSpec((1, k), lambda i, j: (0, 0)),
                  pl.BlockSpec((k, tn), lambda i, j: (0, j))],
        out_specs=pl.BlockSpec((tm, tn), lambda i, j: (i, j)),
        out_shape=jax.ShapeDtypeStruct((m, n), out_dtype),
        scratch_shapes=[pltpu.VMEM((tm, k), BF16)],
        compiler_params=_params(("arbitrary", "arbitrary")),
        name=name,
    )(x, nw.reshape(1, k), w)


def _norm_glu_kernel(x_ref, nw_ref, wg_ref, wu_ref, o_ref, h_ref):
    @pl.when(pl.program_id(1) == 0)
    def _():
        h_ref[...] = _rms_rows(x_ref[...], nw_ref[...]).astype(h_ref.dtype)

    h = h_ref[...]
    g = jnp.dot(h, wg_ref[...], preferred_element_type=F32)
    u = jnp.dot(h, wu_ref[...], preferred_element_type=F32)
    o_ref[...] = (_silu(g) * u).astype(o_ref.dtype)


def norm_glu(x, nw, w, *, tm, tn, name):
    m, k = x.shape
    hidden = w.shape[1] // 2
    tm = min(tm, m)
    nj = hidden // tn
    return pl.pallas_call(
        _norm_glu_kernel,
        grid=(m // tm, nj),
        in_specs=[pl.BlockSpec((tm, k), lambda i, j: (i, 0)),
                  pl.BlockSpec((1, k), lambda i, j: (0, 0)),
                  pl.BlockSpec((k, tn), lambda i, j: (0, j)),
                  pl.BlockSpec((k, tn), lambda i, j: (0, j + nj))],
        out_specs=pl.BlockSpec((tm, tn), lambda i, j: (i, j)),
        out_shape=jax.ShapeDtypeStruct((m, hidden), BF16),
        scratch_shapes=[pltpu.VMEM((tm, k), BF16)],
        compiler_params=_params(("arbitrary", "arbitrary")),
        name=name,
    )(x, nw.reshape(1, k), w, w)


def _mm_res_kernel(a_ref, w_ref, r_ref, o_ref, *, nk):
    p = jnp.dot(a_ref[...], w_ref[...], preferred_element_type=F32)
    if nk == 1:
        o_ref[...] = r_ref[...] + p
    else:
        k = pl.program_id(2)

        @pl.when(k == 0)
        def _():
            o_ref[...] = r_ref[...] + p

        @pl.when(k > 0)
        def _():
            o_ref[...] += p


def matmul_residual(a, w, res, *, tm, tn, tk, name):
    m, k = a.shape
    n = w.shape[1]
    tm = min(tm, m)
    tn = min(tn, n)
    tk = min(tk, k)
    nk = k // tk
    return pl.pallas_call(
        functools.partial(_mm_res_kernel, nk=nk),
        grid=(m // tm, n // tn, nk),
        in_specs=[pl.BlockSpec((tm, tk), lambda i, j, kk: (i, kk)),
                  pl.BlockSpec((tk, tn), lambda i, j, kk: (kk, j)),
                  pl.BlockSpec((tm, tn), lambda i, j, kk: (i, j))],
        out_specs=pl.BlockSpec((tm, tn), lambda i, j, kk: (i, j)),
        out_shape=jax.ShapeDtypeStruct((m, n), F32),
        compiler_params=_params(("arbitrary", "arbitrary", "arbitrary")),
        name=name,
    )(a, w, res)


def _rms_kernel(x_ref, nw_ref, o_ref):
    o_ref[...] = _rms_rows(x_ref[...], nw_ref[...])


def rms_norm_rows(x, nw, *, tm, name):
    m, k = x.shape
    tm = min(tm, m)
    return pl.pallas_call(
        _rms_kernel,
        grid=(m // tm,),
        in_specs=[pl.BlockSpec((tm, k), lambda i: (i, 0)),
                  pl.BlockSpec((1, k), lambda i: (0, 0))],
        out_specs=pl.BlockSpec((tm, k), lambda i: (i, 0)),
        out_shape=jax.ShapeDtypeStruct((m, k), F32),
        compiler_params=_params(("arbitrary",)),
        name=name,
    )(x, nw.reshape(1, k))


def _xattn_kernel(q_ref, k_ref, v_ref, o_ref, *, heads, hd):
    scale = hd ** -0.5
    for h in range(heads):
        sl = slice(h * hd, (h + 1) * hd)
        s = lax.dot_general(q_ref[:, sl], k_ref[:, sl], (((1,), (1,)), ((), ())),
                            preferred_element_type=F32) * scale
        e = jnp.exp(s - jnp.max(s, axis=-1, keepdims=True))
        p = e / jnp.sum(e, axis=-1, keepdims=True)
        o_ref[:, sl] = jnp.dot(p.astype(BF16), v_ref[:, sl],
                               preferred_element_type=F32).astype(o_ref.dtype)


def cross_attention(q, kv, *, batch, heads, tq, name):
    t, d = q.shape
    seq = t // batch
    mtok = kv.shape[0] // batch
    tq = min(tq, seq)
    nq = seq // tq
    return pl.pallas_call(
        functools.partial(_xattn_kernel, heads=heads, hd=d // heads),
        grid=(batch, nq),
        in_specs=[pl.BlockSpec((tq, d), lambda b, i: (b * nq + i, 0)),
                  pl.BlockSpec((mtok, d), lambda b, i: (b, 0)),
                  pl.BlockSpec((mtok, d), lambda b, i: (b, 1))],
        out_specs=pl.BlockSpec((tq, d), lambda b, i: (b * nq + i, 0)),
        out_shape=jax.ShapeDtypeStruct((t, d), BF16),
        compiler_params=_params(("arbitrary", "arbitrary")),
        name=name,
    )(q, kv, kv)


def _ssd_kernel(z_ref, x_ref, b_ref, c_ref, dt_ref,
                cwx_ref, cbx_ref, cwb_ref, cbb_ref, cwc_ref, cbc_ref,
                dtb_ref, alog_ref, dsk_ref, nw_ref, o_ref,
                ext_x, ext_b, ext_c, xs_s, bs_s, cs_s, ac_s, dtg_s, act_s, st_s, y_s,
                *, groups, hpg, hdim, nstate):
    L = CHUNK
    gw = hpg * hdim
    pair_w = 2 * hdim
    assert pair_w == LANES and nstate == LANES and hpg == SUBLANES
    halo = SSD_CONV - 1

    @pl.when(pl.program_id(1) == 0)
    def _():
        st_s[...] = jnp.zeros_like(st_s)
        ext_x[0:SUBLANES, :] = jnp.zeros((SUBLANES, ext_x.shape[1]), F32)
        ext_b[0:SUBLANES, :] = jnp.zeros((SUBLANES, ext_b.shape[1]), F32)
        ext_c[0:SUBLANES, :] = jnp.zeros((SUBLANES, ext_c.shape[1]), F32)

    def conv_silu(ext, src_ref, w_ref, bias_ref, lo, width):
        cols = slice(lo, lo + width)
        ext[SUBLANES:SUBLANES + L, cols] = src_ref[:, cols].astype(F32)
        acc = bias_ref[:, cols]
        for k in range(SSD_CONV):
            start = SUBLANES - halo + k
            acc = acc + w_ref[k:k + 1, cols] * ext[start:start + L, cols]
        ext[0:SUBLANES, cols] = ext[L:L + SUBLANES, cols]
        return _silu(acc)

    for g in range(groups):
        xs_s[g] = conv_silu(ext_x, x_ref, cwx_ref, cbx_ref, g * gw, gw)
        bs_s[g] = conv_silu(ext_b, b_ref, cwb_ref, cbb_ref, g * nstate, nstate).astype(BF16)
        cs_s[g] = conv_silu(ext_c, c_ref, cwc_ref, cbc_ref, g * nstate, nstate).astype(BF16)

    dt_in = dt_ref[...] + dtb_ref[...]
    dt = jnp.maximum(dt_in, 0.0) + jnp.log1p(jnp.exp(-jnp.abs(dt_in)))
    dta = dt * (-jnp.exp(alog_ref[...]))
    row = lax.broadcasted_iota(jnp.int32, (L, LANES), 0)
    acum = dta
    d = 1
    while d < L:
        acum = acum + jnp.where(row >= d, pltpu.roll(acum, d, 0), 0.0)
        d *= 2
    act_s[...] = acum.T
    for g in range(groups):
        shift = (LANES - hpg * g) % LANES
        ac_s[g] = pltpu.roll(acum, shift, 1) if shift else acum
        dtg_s[g] = pltpu.roll(dt, shift, 1) if shift else dt

    lane = lax.broadcasted_iota(jnp.int32, (L, LANES), 1)
    first_head = lane < hdim
    causal = (lax.broadcasted_iota(jnp.int32, (L, L), 0)
              >= lax.broadcasted_iota(jnp.int32, (L, L), 1))

    def group_body(g, carry):
        xs = xs_s[g]
        bc = bs_s[g]
        cc = cs_s[g]
        ac = ac_s[g]
        dtg = dtg_s[g]
        at = act_s[pl.ds(pl.multiple_of(g * hpg, hpg), hpg), :]
        st = st_s[g]
        cb = lax.dot_general(cc, bc, (((1,), (1,)), ((), ())), preferred_element_type=F32)
        ccf = cc.astype(F32)
        alast = ac[L - 1:L, :]
        ys, xws, cds = [], [], []
        for j in range(hpg // 2):
            e0, e1 = 2 * j, 2 * j + 1
            tile = slice(j * LANES, (j + 1) * LANES)
            lhs_parts = []
            for e in (e0, e1):
                acol = ac[:, e:e + 1]
                arow = at[e:e + 1, :]
                decay = jnp.exp(jnp.where(causal, acol - arow, -jnp.inf))
                lhs_parts.append(jnp.concatenate(
                    [(cb * decay).astype(BF16), (ccf * jnp.exp(acol)).astype(BF16)], axis=1))
            lhs = jnp.concatenate(lhs_parts, axis=0)
            dtp = jnp.where(first_head, dtg[:, e0:e0 + 1], dtg[:, e1:e1 + 1])
            xdt = xs[:, tile] * dtp
            rhs = jnp.concatenate([xdt.astype(BF16), st[:, tile].astype(BF16)], axis=0)
            yy = jnp.dot(lhs, rhs, preferred_element_type=F32)
            ys.append(jnp.where(first_head, yy[:L], yy[L:]))
            wp = jnp.where(first_head,
                           jnp.exp(alast[:, e0:e0 + 1] - ac[:, e0:e0 + 1]),
                           jnp.exp(alast[:, e1:e1 + 1] - ac[:, e1:e1 + 1]))
            xws.append((xdt * wp).astype(BF16))
            cds.append(jnp.where(first_head[0:1], jnp.exp(alast[:, e0:e0 + 1]),
                                 jnp.exp(alast[:, e1:e1 + 1])))
        y = jnp.concatenate(ys, axis=1)
        xdtw = jnp.concatenate(xws, axis=1)
        cd = jnp.concatenate(cds, axis=1)
        contrib = lax.dot_general(bc, xdtw, (((0,), (0,)), ((), ())),
                                  preferred_element_type=F32)
        st_s[g] = st * cd + contrib
        y_s[g] = y + xs * dsk_ref[g]
        return carry

    lax.fori_loop(0, groups, group_body, 0)

    for g in range(groups):
        cols = slice(g * gw, (g + 1) * gw)
        y = y_s[g] * _silu(z_ref[:, cols].astype(F32))
        y = y * lax.rsqrt(jnp.mean(y * y, axis=-1, keepdims=True) + NORM_EPS)
        o_ref[:, cols] = (y * nw_ref[:, cols]).astype(o_ref.dtype)


def ssd_scan(zx, dt_raw, conv_w, conv_b, dt_bias, a_log, d_skip, norm_w, *, batch, name):
    t = zx.shape[0]
    seq = t // batch
    nc = seq // CHUNK
    groups, nstate, hdim = SSD_GROUPS, SSD_STATE, SSD_HEAD_DIM
    d_inner = norm_w.shape[0]
    heads = d_inner // hdim
    hpg = heads // groups
    gw = hpg * hdim
    gn = groups * nstate
    assert d_inner % gn == 0 and zx.shape[1] == 2 * d_inner + 2 * gn
    nb = d_inner // gn

    cwx, cwb, cwc = conv_w[:, :d_inner], conv_w[:, d_inner:d_inner + gn], conv_w[:, d_inner + gn:]
    cbx, cbb, cbc = (conv_b[None, :d_inner], conv_b[None, d_inner:d_inner + gn],
                     conv_b[None, d_inner + gn:])
    pad = LANES - heads
    dtb = jnp.pad(dt_bias, (0, pad)).reshape(1, LANES)
    alog = jnp.pad(a_log, (0, pad)).reshape(1, LANES)
    dsk = jnp.repeat(d_skip, hdim).reshape(groups, 1, gw)
    nw = norm_w.reshape(1, d_inner)

    row = lambda b, c: b * nc + c
    full = lambda shape: pl.BlockSpec(shape, lambda b, c: (0,) * len(shape))
    return pl.pallas_call(
        functools.partial(_ssd_kernel, groups=groups, hpg=hpg, hdim=hdim, nstate=nstate),
        grid=(batch, nc),
        in_specs=[pl.BlockSpec((CHUNK, d_inner), lambda b, c: (row(b, c), 0)),
                  pl.BlockSpec((CHUNK, d_inner), lambda b, c: (row(b, c), 1)),
                  pl.BlockSpec((CHUNK, gn), lambda b, c: (row(b, c), 2 * nb)),
                  pl.BlockSpec((CHUNK, gn), lambda b, c: (row(b, c), 2 * nb + 1)),
                  pl.BlockSpec((CHUNK, LANES), lambda b, c: (row(b, c), 0)),
                  full((SSD_CONV, d_inner)), full((1, d_inner)),
                  full((SSD_CONV, gn)), full((1, gn)),
                  full((SSD_CONV, gn)), full((1, gn)),
                  full((1, LANES)), full((1, LANES)),
                  full((groups, 1, gw)), full((1, d_inner))],
        out_specs=pl.BlockSpec((CHUNK, d_inner), lambda b, c: (row(b, c), 0)),
        out_shape=jax.ShapeDtypeStruct((t, d_inner), BF16),
        scratch_shapes=[pltpu.VMEM((CHUNK + SUBLANES, d_inner), F32),
                        pltpu.VMEM((CHUNK + SUBLANES, gn), F32),
                        pltpu.VMEM((CHUNK + SUBLANES, gn), F32),
                        pltpu.VMEM((groups, CHUNK, gw), F32),
                        pltpu.VMEM((groups, CHUNK, nstate), BF16),
                        pltpu.VMEM((groups, CHUNK, nstate), BF16),
                        pltpu.VMEM((groups, CHUNK, LANES), F32),
                        pltpu.VMEM((groups, CHUNK, LANES), F32),
                        pltpu.VMEM((LANES, CHUNK), F32),
                        pltpu.VMEM((groups, nstate, gw), F32),
                        pltpu.VMEM((groups, CHUNK, gw), F32)],
        compiler_params=_params(("arbitrary", "arbitrary")),
        name=name,
    )(zx, zx, zx, zx, dt_raw, cwx, cbx, cwb, cbb, cwc, cbc, dtb, alog, dsk, nw)


def _ret_kernel(q_ref, k_ref, v_ref, g_ref, cos_ref, sin_ref,
                idec_ref, qdec_ref, kdec_ref, cdec_ref, o_ref, st_s, *, heads, dk, dv):
    @pl.when(pl.program_id(1) == 0)
    def _():
        st_s[...] = jnp.zeros_like(st_s)

    cos = cos_ref[...]
    sin = sin_ref[...]
    half = dk // 2

    def rope(ref, h):
        a = ref[:, h * dk:h * dk + half].astype(F32)
        b = ref[:, h * dk + half:(h + 1) * dk].astype(F32)
        return jnp.concatenate([a * cos - b * sin, b * cos + a * sin], axis=1)

    for h in range(heads):
        qr = rope(q_ref, h)
        kr = rope(k_ref, h) * dk ** -0.5
        v = v_ref[:, h * dv:(h + 1) * dv]
        s = lax.dot_general(qr.astype(BF16), kr.astype(BF16), (((1,), (1,)), ((), ())),
                            preferred_element_type=F32) * idec_ref[h]
        intra = jnp.dot(s.astype(BF16), v, preferred_element_type=F32)
        st = st_s[h]
        cross = jnp.dot((qr * qdec_ref[h]).astype(BF16), st.astype(BF16),
                        preferred_element_type=F32)
        st_s[h] = st * cdec_ref[h] + lax.dot_general(
            (kr * kdec_ref[h]).astype(BF16), v, (((0,), (0,)), ((), ())),
            preferred_element_type=F32)
        y = intra + cross
        y = y * lax.rsqrt(jnp.mean(y * y, axis=-1, keepdims=True) + NORM_EPS)
        gate = g_ref[:, h * dv:(h + 1) * dv].astype(F32)
        o_ref[:, h * dv:(h + 1) * dv] = (y * _silu(gate)).astype(o_ref.dtype)


def retention_scan(proj, *, batch, qk_dim, v_dim, name):
    t = proj.shape[0]
    seq = t // batch
    nc = seq // CHUNK
    heads = RET_HEADS
    dk, dv = qk_dim // heads, v_dim // heads
    assert v_dim == 2 * qk_dim and proj.shape[1] == 2 * qk_dim + 2 * v_dim

    pos = jnp.arange(seq, dtype=F32)
    inv_freq = 1.0 / (ROPE_BASE ** jnp.linspace(0.0, 1.0, dk // 2, dtype=F32))
    theta = pos[:, None] * inv_freq[None, :]
    cos, sin = jnp.cos(theta), jnp.sin(theta)
    log_gamma = jnp.log1p(-jnp.exp2(-5.0 - jnp.arange(heads, dtype=F32)))
    idx = jnp.arange(CHUNK, dtype=F32)
    causal = jnp.tril(jnp.ones((CHUNK, CHUNK), dtype=bool))
    idec = jnp.exp(jnp.where(causal[None],
                             (idx[:, None] - idx[None, :])[None] * log_gamma[:, None, None],
                             -jnp.inf))
    qdec = jnp.exp(log_gamma[:, None] * (idx + 1.0)[None, :])[..., None]
    kdec = jnp.exp(log_gamma[:, None] * (CHUNK - 1.0 - idx)[None, :])[..., None]
    cdec = jnp.exp(CHUNK * log_gamma)[:, None, None]

    row = lambda b, c: b * nc + c
    full = lambda shape: pl.BlockSpec(shape, lambda b, c: (0,) * len(shape))
    return pl.pallas_call(
        functools.partial(_ret_kernel, heads=heads, dk=dk, dv=dv),
        grid=(batch, nc),
        in_specs=[pl.BlockSpec((CHUNK, qk_dim), lambda b, c: (row(b, c), 0)),
                  pl.BlockSpec((CHUNK, qk_dim), lambda b, c: (row(b, c), 1)),
                  pl.BlockSpec((CHUNK, v_dim), lambda b, c: (row(b, c), 1)),
                  pl.BlockSpec((CHUNK, v_dim), lambda b, c: (row(b, c), 2)),
                  pl.BlockSpec((CHUNK, dk // 2), lambda b, c: (c, 0)),
                  pl.BlockSpec((CHUNK, dk // 2), lambda b, c: (c, 0)),
                  full((heads, CHUNK, CHUNK)), full((heads, CHUNK, 1)),
                  full((heads, CHUNK, 1)), full((heads, 1, 1))],
        out_specs=pl.BlockSpec((CHUNK, v_dim), lambda b, c: (row(b, c), 0)),
        out_shape=jax.ShapeDtypeStruct((t, v_dim), BF16),
        scratch_shapes=[pltpu.VMEM((heads, dk, dv), F32)],
        compiler_params=_params(("arbitrary", "arbitrary")),
        name=name,
    )(proj, proj, proj, proj, cos, sin, idec, qdec, kdec, cdec)


def _deinterleave_heads(w, heads):
    k, n = w.shape
    return w.reshape(k, heads, n // heads // 2, 2).transpose(0, 1, 3, 2).reshape(k, n)


def kernel(x, mem, ssd_in_w, ssd_conv_w, ssd_conv_b, ssd_dt_bias, ssd_a_log, ssd_d, ssd_norm_w,
           ssd_out_w, ret_in_w, ret_out_w, mixer_norm_w, mem_q_norm_w, mem_kv_norm_w, mem_q_w,
           mem_kv_w, mem_o_w, ffn_norm_w, ffn_in_w, ffn_out_w, final_norm_w):
    batch, seq, d = x.shape
    depth = mixer_norm_w.shape[0]
    xf = x.reshape(batch * seq, d)
    memf = mem.reshape(-1, d)
    ssd_inner = ssd_norm_w.shape[1]
    ssd_main = 2 * ssd_inner + 2 * SSD_GROUPS * SSD_STATE
    ssd_heads = ssd_inner // SSD_HEAD_DIM
    ret_qk = ret_in_w.shape[2] // 6
    ret_v = 2 * ret_qk

    for i in range(depth):
        j = i // 2
        if i % 2 == 0:
            w_in = ssd_in_w[j]
            zx = norm_matmul(xf, mixer_norm_w[i], w_in[:, :ssd_main].astype(BF16),
                             tm=1024, tn=1024, out_dtype=BF16, name=f"ssd_in_{i}")
            w_dt = jnp.pad(w_in[:, ssd_main:], ((0, 0), (0, LANES - ssd_heads))).astype(BF16)
            dt_raw = norm_matmul(xf, mixer_norm_w[i], w_dt, tm=1024, tn=LANES,
                                 out_dtype=F32, name=f"ssd_dt_{i}")
            y = ssd_scan(zx, dt_raw, ssd_conv_w[j], ssd_conv_b[j], ssd_dt_bias[j], ssd_a_log[j],
                         ssd_d[j], ssd_norm_w[j], batch=batch, name=f"ssd_scan_{i}")
            xf = matmul_residual(y, ssd_out_w[j].astype(BF16), xf, tm=1024, tn=1024, tk=2048,
                                 name=f"ssd_out_{i}")
        else:
            w_in = ret_in_w[j]
            w_in = jnp.concatenate(
                [_deinterleave_heads(w_in[:, :ret_qk], RET_HEADS),
                 _deinterleave_heads(w_in[:, ret_qk:2 * ret_qk], RET_HEADS),
                 w_in[:, 2 * ret_qk:]], axis=1).astype(BF16)
            proj = norm_matmul(xf, mixer_norm_w[i], w_in, tm=1024, tn=1024, out_dtype=BF16,
                               name=f"ret_in_{i}")
            y = retention_scan(proj, batch=batch, qk_dim=ret_qk, v_dim=ret_v,
                               name=f"ret_scan_{i}")
            xf = matmul_residual(y, ret_out_w[j].astype(BF16), xf, tm=1024, tn=1024, tk=2048,
                                 name=f"ret_out_{i}")

        q = norm_matmul(xf, mem_q_norm_w[i], mem_q_w[i].astype(BF16), tm=1024, tn=1024,
                        out_dtype=BF16, name=f"mem_q_{i}")
        kv = norm_matmul(memf, mem_kv_norm_w[i], mem_kv_w[i].astype(BF16), tm=512, tn=1024,
                         out_dtype=BF16, name=f"mem_kv_{i}")
        o = cross_attention(q, kv, batch=batch, heads=MEM_HEADS, tq=512, name=f"mem_attn_{i}")
        xf = matmul_residual(o, mem_o_w[i].astype(BF16), xf, tm=1024, tn=1024, tk=2048,
                             name=f"mem_o_{i}")

        hid = norm_glu(xf, ffn_norm_w[i], ffn_in_w[i].astype(BF16), tm=1024, tn=512,
                       name=f"ffn_in_{i}")
        xf = matmul_residual(hid, ffn_out_w[i].astype(BF16), xf, tm=1024, tn=1024,
                             tk=hid.shape[1] // 2, name=f"ffn_out_{i}")

    out = rms_norm_rows(xf, final_norm_w, tm=512, name="final_norm")
    return out.reshape(batch, seq, d)
```

```python
import functools
import math

import jax
import jax.numpy as jnp
from jax import lax
from jax.experimental import pallas as pl
from jax.experimental.pallas import tpu as pltpu

NORM_EPS = 1e-6
CHUNK = 128
LANES = 128
SUBLANES = 8
VMEM_LIMIT_BYTES = 56 * 1024 * 1024

SSD_HEAD_DIM = 64
SSD_GROUPS = 8
SSD_STATE = 128
SSD_CONV = 4
RET_HEADS = 8
MEM_HEADS = 4
ROPE_BASE = 10000.0

F32 = jnp.float32
BF16 = jnp.bfloat16


def _params(semantics):
    return pltpu.CompilerParams(dimension_semantics=semantics,
                                vmem_limit_bytes=VMEM_LIMIT_BYTES)


def _silu(v):
    return v * jax.nn.sigmoid(v)


def _rms_rows(x, w):
    ms = jnp.mean(x * x, axis=-1, keepdims=True)
    return x * lax.rsqrt(ms + NORM_EPS) * w


def _norm_prologue(x_ref, nw_ref, h_ref):
    @pl.when(pl.program_id(1) == 0)
    def _():
        h_ref[...] = _rms_rows(x_ref[...], nw_ref[...]).astype(h_ref.dtype)


def _norm_mm_kernel(x_ref, nw_ref, w_ref, o_ref, h_ref):
    _norm_prologue(x_ref, nw_ref, h_ref)
    o_ref[...] = jnp.dot(h_ref[...], w_ref[...].astype(BF16),
                         preferred_element_type=F32).astype(o_ref.dtype)


def norm_matmul(x, nw, w, layer, *, n_out, tm, tn, out_dtype, name):
    m, k = x.shape
    tm = min(tm, m)
    return pl.pallas_call(
        _norm_mm_kernel,
        grid=(m // tm, n_out // tn),
        in_specs=[pl.BlockSpec((tm, k), lambda i, j: (i, 0)),
                  pl.BlockSpec((1, k), lambda i, j: (0, 0)),
                  pl.BlockSpec((None, k, tn), lambda i, j: (layer, 0, j))],
        out_specs=pl.BlockSpec((tm, tn), lambda i, j: (i, j)),
        out_shape=jax.ShapeDtypeStruct((m, n_out), out_dtype),
        scratch_shapes=[pltpu.VMEM((tm, k), BF16)],
        compiler_params=_params(("arbitrary", "arbitrary")),
        name=name,
    )(x, nw.reshape(1, k), w)


def _norm_mm_split_kernel(x_ref, nw_ref, wa_ref, wb_ref, o_ref, h_ref, *, na):
    _norm_prologue(x_ref, nw_ref, h_ref)
    j = pl.program_id(1)

    @pl.when(j < na)
    def _():
        o_ref[...] = jnp.dot(h_ref[...], wa_ref[...].astype(BF16),
                             preferred_element_type=F32).astype(o_ref.dtype)

    @pl.when(j >= na)
    def _():
        o_ref[...] = jnp.dot(h_ref[...], wb_ref[...].astype(BF16),
                             preferred_element_type=F32).astype(o_ref.dtype)


def norm_matmul_split(x, nw, wa, w, layer, *, tm, tn, out_dtype, name):
    m, k = x.shape
    n_out = w.shape[2]
    na = wa.shape[1] // tn
    tm = min(tm, m)
    return pl.pallas_call(
        functools.partial(_norm_mm_split_kernel, na=na),
        grid=(m // tm, n_out // tn),
        in_specs=[pl.BlockSpec((tm, k), lambda i, j: (i, 0)),
                  pl.BlockSpec((1, k), lambda i, j: (0, 0)),
                  pl.BlockSpec((k, tn), lambda i, j: (0, jnp.minimum(j, na - 1))),
                  pl.BlockSpec((None, k, tn), lambda i, j: (layer, 0, jnp.maximum(j, na)))],
        out_specs=pl.BlockSpec((tm, tn), lambda i, j: (i, j)),
        out_shape=jax.ShapeDtypeStruct((m, n_out), out_dtype),
        scratch_shapes=[pltpu.VMEM((tm, k), BF16)],
        compiler_params=_params(("arbitrary", "arbitrary")),
        name=name,
    )(x, nw.reshape(1, k), wa, w)


def _norm_glu_kernel(x_ref, nw_ref, wg_ref, wu_ref, o_ref, h_ref):
    _norm_prologue(x_ref, nw_ref, h_ref)
    h = h_ref[...]
    g = jnp.dot(h, wg_ref[...].astype(BF16), preferred_element_type=F32)
    u = jnp.dot(h, wu_ref[...].astype(BF16), preferred_element_type=F32)
    o_ref[...] = (_silu(g) * u).astype(o_ref.dtype)


def norm_glu(x, nw, w, layer, *, tm, tn, name):
    m, k = x.shape
    hidden = w.shape[2] // 2
    tm = min(tm, m)
    nj = hidden // tn
    return pl.pallas_call(
        _norm_glu_kernel,
        grid=(m // tm, nj),
        in_specs=[pl.BlockSpec((tm, k), lambda i, j: (i, 0)),
                  pl.BlockSpec((1, k), lambda i, j: (0, 0)),
                  pl.BlockSpec((None, k, tn), lambda i, j: (layer, 0, j)),
                  pl.BlockSpec((None, k, tn), lambda i, j: (layer, 0, j + nj))],
        out_specs=pl.BlockSpec((tm, tn), lambda i, j: (i, j)),
        out_shape=jax.ShapeDtypeStruct((m, hidden), BF16),
        scratch_shapes=[pltpu.VMEM((tm, k), BF16)],
        compiler_params=_params(("arbitrary", "arbitrary")),
        name=name,
    )(x, nw.reshape(1, k), w, w)


def _mm_res_kernel(a_ref, w_ref, r_ref, o_ref, *, nk):
    p = jnp.dot(a_ref[...], w_ref[...].astype(BF16), preferred_element_type=F32)
    if nk == 1:
        o_ref[...] = r_ref[...] + p
    else:
        k = pl.program_id(2)

        @pl.when(k == 0)
        def _():
            o_ref[...] = r_ref[...] + p

        @pl.when(k > 0)
        def _():
            o_ref[...] += p


def matmul_residual(a, w, layer, res, *, tm, tn, tk, name):
    m, k = a.shape
    n = w.shape[2]
    tm = min(tm, m)
    nk = k // tk
    return pl.pallas_call(
        functools.partial(_mm_res_kernel, nk=nk),
        grid=(m // tm, n // tn, nk),
        in_specs=[pl.BlockSpec((tm, tk), lambda i, j, kk: (i, kk)),
                  pl.BlockSpec((None, tk, tn), lambda i, j, kk: (layer, kk, j)),
                  pl.BlockSpec((tm, tn), lambda i, j, kk: (i, j))],
        out_specs=pl.BlockSpec((tm, tn), lambda i, j, kk: (i, j)),
        out_shape=jax.ShapeDtypeStruct((m, n), F32),
        compiler_params=_params(("arbitrary", "arbitrary", "arbitrary")),
        name=name,
    )(a, w, res)


def _rms_kernel(x_ref, nw_ref, o_ref):
    o_ref[...] = _rms_rows(x_ref[...], nw_ref[...])


def rms_norm_rows(x, nw, *, tm, name):
    m, k = x.shape
    tm = min(tm, m)
    return pl.pallas_call(
        _rms_kernel,
        grid=(m // tm,),
        in_specs=[pl.BlockSpec((tm, k), lambda i: (i, 0)),
                  pl.BlockSpec((1, k), lambda i: (0, 0))],
        out_specs=pl.BlockSpec((tm, k), lambda i: (i, 0)),
        out_shape=jax.ShapeDtypeStruct((m, k), F32),
        compiler_params=_params(("arbitrary",)),
        name=name,
    )(x, nw.reshape(1, k))


def _xattn_kernel(q_ref, k_ref, v_ref, o_ref, *, heads, hd):
    scale = hd ** -0.5
    for h in range(heads):
        sl = slice(h * hd, (h + 1) * hd)
        s = lax.dot_general(q_ref[:, sl], k_ref[:, sl], (((1,), (1,)), ((), ())),
                            preferred_element_type=F32) * scale
        e = jnp.exp(s - jnp.max(s, axis=-1, keepdims=True))
        p = e / jnp.sum(e, axis=-1, keepdims=True)
        o_ref[:, sl] = jnp.dot(p.astype(BF16), v_ref[:, sl],
                               preferred_element_type=F32).astype(o_ref.dtype)


def cross_attention(q, kv, *, batch, heads, tq, name):
    t, d = q.shape
    seq = t // batch
    mtok = kv.shape[0] // batch
    tq = min(tq, seq)
    nq = seq // tq
    return pl.pallas_call(
        functools.partial(_xattn_kernel, heads=heads, hd=d // heads),
        grid=(batch, nq),
        in_specs=[pl.BlockSpec((tq, d), lambda b, i: (b * nq + i, 0)),
                  pl.BlockSpec((mtok, d), lambda b, i: (b, 0)),
                  pl.BlockSpec((mtok, d), lambda b, i: (b, 1))],
        out_specs=pl.BlockSpec((tq, d), lambda b, i: (b * nq + i, 0)),
        out_shape=jax.ShapeDtypeStruct((t, d), BF16),
        compiler_params=_params(("arbitrary", "arbitrary")),
        name=name,
    )(q, kv, kv)


def _ssd_kernel(z_ref, x_ref, b_ref, c_ref, dt_ref,
                cwx_ref, cbx_ref, cwb_ref, cbb_ref, cwc_ref, cbc_ref,
                dtb_ref, alog_ref, dsk_ref, nw_ref, o_ref,
                ext_x, ext_b, ext_c, xs_s, bs_s, cs_s, ac_s, dtg_s, act_s, st_s, y_s,
                *, groups, hpg, hdim, nstate):
    L = CHUNK
    gw = hpg * hdim
    pair_w = 2 * hdim
    assert pair_w == LANES and nstate == LANES and hpg == SUBLANES
    halo = SSD_CONV - 1

    @pl.when(pl.program_id(1) == 0)
    def _():
        st_s[...] = jnp.zeros_like(st_s)
        ext_x[0:SUBLANES, :] = jnp.zeros((SUBLANES, ext_x.shape[1]), F32)
        ext_b[0:SUBLANES, :] = jnp.zeros((SUBLANES, ext_b.shape[1]), F32)
        ext_c[0:SUBLANES, :] = jnp.zeros((SUBLANES, ext_c.shape[1]), F32)

    def conv_silu(ext, src_ref, w_ref, bias_ref, lo, width):
        cols = slice(lo, lo + width)
        ext[SUBLANES:SUBLANES + L, cols] = src_ref[:, cols].astype(F32)
        acc = bias_ref[:, cols]
        for k in range(SSD_CONV):
            start = SUBLANES - halo + k
            acc = acc + w_ref[k:k + 1, cols] * ext[start:start + L, cols]
        ext[0:SUBLANES, cols] = ext[L:L + SUBLANES, cols]
        return _silu(acc)

    for g in range(groups):
        xs_s[g] = conv_silu(ext_x, x_ref, cwx_ref, cbx_ref, g * gw, gw)
        bs_s[g] = conv_silu(ext_b, b_ref, cwb_ref, cbb_ref, g * nstate, nstate).astype(BF16)
        cs_s[g] = conv_silu(ext_c, c_ref, cwc_ref, cbc_ref, g * nstate, nstate).astype(BF16)

    dt_in = dt_ref[...] + dtb_ref[...]
    dt = jnp.maximum(dt_in, 0.0) + jnp.log1p(jnp.exp(-jnp.abs(dt_in)))
    dta = dt * (-jnp.exp(alog_ref[...]))
    row = lax.broadcasted_iota(jnp.int32, (L, LANES), 0)
    acum = dta
    d = 1
    while d < L:
        acum = acum + jnp.where(row >= d, pltpu.roll(acum, d, 0), 0.0)
        d *= 2
    act_s[...] = acum.T
    for g in range(groups):
        shift = (LANES - hpg * g) % LANES
        ac_s[g] = pltpu.roll(acum, shift, 1) if shift else acum
        dtg_s[g] = pltpu.roll(dt, shift, 1) if shift else dt

    lane = lax.broadcasted_iota(jnp.int32, (L, LANES), 1)
    first_head = lane < hdim
    causal = (lax.broadcasted_iota(jnp.int32, (L, L), 0)
              >= lax.broadcasted_iota(jnp.int32, (L, L), 1))

    def group_body(g, carry):
        xs = xs_s[g]
        bc = bs_s[g]
        cc = cs_s[g]
        ac = ac_s[g]
        dtg = dtg_s[g]
        at = act_s[pl.ds(pl.multiple_of(g * hpg, hpg), hpg), :]
        st = st_s[g]
        cb = lax.dot_general(cc, bc, (((1,), (1,)), ((), ())), preferred_element_type=F32)
        ccf = cc.astype(F32)
        alast = ac[L - 1:L, :]
        ys, xws, cds = [], [], []
        for j in range(hpg // 2):
            e0, e1 = 2 * j, 2 * j + 1
            tile = slice(j * LANES, (j + 1) * LANES)
            lhs_parts = []
            for e in (e0, e1):
                acol = ac[:, e:e + 1]
                arow = at[e:e + 1, :]
                decay = jnp.exp(jnp.where(causal, acol - arow, -jnp.inf))
                lhs_parts.append(jnp.concatenate(
                    [(cb * decay).astype(BF16), (ccf * jnp.exp(acol)).astype(BF16)], axis=1))
            lhs = jnp.concatenate(lhs_parts, axis=0)
            dtp = jnp.where(first_head, dtg[:, e0:e0 + 1], dtg[:, e1:e1 + 1])
            xdt = xs[:, tile] * dtp
            rhs = jnp.concatenate([xdt.astype(BF16), st[:, tile].astype(BF16)], axis=0)
            yy = jnp.dot(lhs, rhs, preferred_element_type=F32)
            ys.append(jnp.where(first_head, yy[:L], yy[L:]))
            wp = jnp.where(first_head,
                           jnp.exp(alast[:, e0:e0 + 1] - ac[:, e0:e0 + 1]),
                           jnp.exp(alast[:, e1:e1 + 1] - ac[:, e1:e1 + 1]))
            xws.append((xdt * wp).astype(BF16))
            cds.append(jnp.where(first_head[0:1], jnp.exp(alast[:, e0:e0 + 1]),
                                 jnp.exp(alast[:, e1:e1 + 1])))
        y = jnp.concatenate(ys, axis=1)
        xdtw = jnp.concatenate(xws, axis=1)
        cd = jnp.concatenate(cds, axis=1)
        contrib = lax.dot_general(bc, xdtw, (((0,), (0,)), ((), ())),
                                  preferred_element_type=F32)
        st_s[g] = st * cd + contrib
        y_s[g] = y + xs * dsk_ref[g]
        return carry

    lax.fori_loop(0, groups, group_body, 0)

    for g in range(groups):
        cols = slice(g * gw, (g + 1) * gw)
        y = y_s[g] * _silu(z_ref[:, cols].astype(F32))
        y = y * lax.rsqrt(jnp.mean(y * y, axis=-1, keepdims=True) + NORM_EPS)
        o_ref[:, cols] = (y * nw_ref[:, cols]).astype(o_ref.dtype)


def ssd_scan(zx, dt_raw, conv_w, conv_b, dt_bias, a_log, d_skip, norm_w, *, batch, name):
    t = zx.shape[0]
    seq = t // batch
    nc = seq // CHUNK
    groups, nstate, hdim = SSD_GROUPS, SSD_STATE, SSD_HEAD_DIM
    d_inner = norm_w.shape[0]
    heads = d_inner // hdim
    hpg = heads // groups
    gw = hpg * hdim
    gn = groups * nstate
    assert d_inner % gn == 0 and zx.shape[1] == 2 * d_inner + 2 * gn
    nb = d_inner // gn

    cwx, cwb, cwc = conv_w[:, :d_inner], conv_w[:, d_inner:d_inner + gn], conv_w[:, d_inner + gn:]
    cbx, cbb, cbc = (conv_b[None, :d_inner], conv_b[None, d_inner:d_inner + gn],
                     conv_b[None, d_inner + gn:])
    pad = LANES - heads
    dtb = jnp.pad(dt_bias, (0, pad)).reshape(1, LANES)
    alog = jnp.pad(a_log, (0, pad)).reshape(1, LANES)
    dsk = jnp.repeat(d_skip, hdim).reshape(groups, 1, gw)
    nw = norm_w.reshape(1, d_inner)

    row = lambda b, c: b * nc + c
    full = lambda shape: pl.BlockSpec(shape, lambda b, c: (0,) * len(shape))
    return pl.pallas_call(
        functools.partial(_ssd_kernel, groups=groups, hpg=hpg, hdim=hdim, nstate=nstate),
        grid=(batch, nc),
        in_specs=[pl.BlockSpec((CHUNK, d_inner), lambda b, c: (row(b, c), 0)),
                  pl.BlockSpec((CHUNK, d_inner), lambda b, c: (row(b, c), 1)),
                  pl.BlockSpec((CHUNK, gn), lambda b, c: (row(b, c), 2 * nb)),
                  pl.BlockSpec((CHUNK, gn), lambda b, c: (row(b, c), 2 * nb + 1)),
                  pl.BlockSpec((CHUNK, LANES), lambda b, c: (row(b, c), 0)),
                  full((SSD_CONV, d_inner)), full((1, d_inner)),
                  full((SSD_CONV, gn)), full((1, gn)),
                  full((SSD_CONV, gn)), full((1, gn)),
                  full((1, LANES)), full((1, LANES)),
                  full((groups, 1, gw)), full((1, d_inner))],
        out_specs=pl.BlockSpec((CHUNK, d_inner), lambda b, c: (row(b, c), 0)),
        out_shape=jax.ShapeDtypeStruct((t, d_inner), BF16),
        scratch_shapes=[pltpu.VMEM((CHUNK + SUBLANES, d_inner), F32),
                        pltpu.VMEM((CHUNK + SUBLANES, gn), F32),
                        pltpu.VMEM((CHUNK + SUBLANES, gn), F32),
                        pltpu.VMEM((groups, CHUNK, gw), F32),
                        pltpu.VMEM((groups, CHUNK, nstate), BF16),
                        pltpu.VMEM((groups, CHUNK, nstate), BF16),
                        pltpu.VMEM((groups, CHUNK, LANES), F32),
                        pltpu.VMEM((groups, CHUNK, LANES), F32),
                        pltpu.VMEM((LANES, CHUNK), F32),
                        pltpu.VMEM((groups, nstate, gw), F32),
                        pltpu.VMEM((groups, CHUNK, gw), F32)],
        compiler_params=_params(("arbitrary", "arbitrary")),
        name=name,
    )(zx, zx, zx, zx, dt_raw, cwx, cbx, cwb, cbb, cwc, cbc, dtb, alog, dsk, nw)


def _ret_kernel(q_ref, k_ref, v_ref, g_ref, cos_ref, sin_ref,
                idec_ref, qdec_ref, kdec_ref, cdec_ref, o_ref, st_s, *, heads, dk, dv):
    @pl.when(pl.program_id(1) == 0)
    def _():
        st_s[...] = jnp.zeros_like(st_s)

    cos = cos_ref[...]
    sin = sin_ref[...]
    half = dk // 2

    def rope(ref, h):
        a = ref[:, h * dk:h * dk + half].astype(F32)
        b = ref[:, h * dk + half:(h + 1) * dk].astype(F32)
        return jnp.concatenate([a * cos - b * sin, b * cos + a * sin], axis=1)

    for h in range(heads):
        qr = rope(q_ref, h)
        kr = rope(k_ref, h) * dk ** -0.5
        v = v_ref[:, h * dv:(h + 1) * dv]
        s = lax.dot_general(qr.astype(BF16), kr.astype(BF16), (((1,), (1,)), ((), ())),
                            preferred_element_type=F32) * idec_ref[h]
        intra = jnp.dot(s.astype(BF16), v, preferred_element_type=F32)
        st = st_s[h]
        cross = jnp.dot((qr * qdec_ref[h]).astype(BF16), st.astype(BF16),
                        preferred_element_type=F32)
        st_s[h] = st * cdec_ref[h] + lax.dot_general(
            (kr * kdec_ref[h]).astype(BF16), v, (((0,), (0,)), ((), ())),
            preferred_element_type=F32)
        y = intra + cross
        y = y * lax.rsqrt(jnp.mean(y * y, axis=-1, keepdims=True) + NORM_EPS)
        gate = g_ref[:, h * dv:(h + 1) * dv].astype(F32)
        o_ref[:, h * dv:(h + 1) * dv] = (y * _silu(gate)).astype(o_ref.dtype)


def retention_scan(proj, *, batch, qk_dim, v_dim, name):
    t = proj.shape[0]
    seq = t // batch
    nc = seq // CHUNK
    heads = RET_HEADS
    dk, dv = qk_dim // heads, v_dim // heads
    assert v_dim == 2 * qk_dim and proj.shape[1] == 2 * qk_dim + 2 * v_dim

    pos = jnp.arange(seq, dtype=F32)
    inv_freq = 1.0 / (ROPE_BASE ** jnp.linspace(0.0, 1.0, dk // 2, dtype=F32))
    theta = pos[:, None] * inv_freq[None, :]
    cos, sin = jnp.cos(theta), jnp.sin(theta)
    log_gamma = jnp.log1p(-jnp.exp2(-5.0 - jnp.arange(heads, dtype=F32)))
    idx = jnp.arange(CHUNK, dtype=F32)
    causal = jnp.tril(jnp.ones((CHUNK, CHUNK), dtype=bool))
    idec = jnp.exp(jnp.where(causal[None],
                             (idx[:, None] - idx[None, :])[None] * log_gamma[:, None, None],
                             -jnp.inf))
    qdec = jnp.exp(log_gamma[:, None] * (idx + 1.0)[None, :])[..., None]
    kdec = jnp.exp(log_gamma[:, None] * (CHUNK - 1.0 - idx)[None, :])[..., None]
    cdec = jnp.exp(CHUNK * log_gamma)[:, None, None]

    row = lambda b, c: b * nc + c
    full = lambda shape: pl.BlockSpec(shape, lambda b, c: (0,) * len(shape))
    return pl.pallas_call(
        functools.partial(_ret_kernel, heads=heads, dk=dk, dv=dv),
        grid=(batch, nc),
        in_specs=[pl.BlockSpec((CHUNK, qk_dim), lambda b, c: (row(b, c), 0)),
                  pl.BlockSpec((CHUNK, qk_dim), lambda b, c: (row(b, c), 1)),
                  pl.BlockSpec((CHUNK, v_dim), lambda b, c: (row(b, c), 1)),
                  pl.BlockSpec((CHUNK, v_dim), lambda b, c: (row(b, c), 2)),
                  pl.BlockSpec((CHUNK, dk // 2), lambda b, c: (c, 0)),
                  pl.BlockSpec((CHUNK, dk // 2), lambda b, c: (c, 0)),
                  full((heads, CHUNK, CHUNK)), full((heads, CHUNK, 1)),
                  full((heads, CHUNK, 1)), full((heads, 1, 1))],
        out_specs=pl.BlockSpec((CHUNK, v_dim), lambda b, c: (row(b, c), 0)),
        out_shape=jax.ShapeDtypeStruct((t, v_dim), BF16),
        scratch_shapes=[pltpu.VMEM((heads, dk, dv), F32)],
        compiler_params=_params(("arbitrary", "arbitrary")),
        name=name,
    )(proj, proj, proj, proj, cos, sin, idec, qdec, kdec, cdec)


def _deinterleave_heads(w, heads):
    k, n = w.shape
    return w.reshape(k, heads, n // heads // 2, 2).transpose(0, 1, 3, 2).reshape(k, n)


def kernel(x, mem, ssd_in_w, ssd_conv_w, ssd_conv_b, ssd_dt_bias, ssd_a_log, ssd_d, ssd_norm_w,
           ssd_out_w, ret_in_w, ret_out_w, mixer_norm_w, mem_q_norm_w, mem_kv_norm_w, mem_q_w,
           mem_kv_w, mem_o_w, ffn_norm_w, ffn_in_w, ffn_out_w, final_norm_w):
    batch, seq, d = x.shape
    depth = mixer_norm_w.shape[0]
    xf = x.reshape(batch * seq, d)
    memf = mem.reshape(-1, d)
    ssd_inner = ssd_norm_w.shape[1]
    ssd_main = 2 * ssd_inner + 2 * SSD_GROUPS * SSD_STATE
    ssd_heads = ssd_inner // SSD_HEAD_DIM
    ret_qk = ret_in_w.shape[2] // 6
    ret_v = 2 * ret_qk

    for i in range(depth):
        j = i // 2
        if i % 2 == 0:
            zx = norm_matmul(xf, mixer_norm_w[i], ssd_in_w, j, n_out=ssd_main, tm=1024, tn=1024,
                             out_dtype=BF16, name=f"ssd_in_{i}")
            w_dt = jnp.pad(ssd_in_w[j, :, ssd_main:], ((0, 0), (0, LANES - ssd_heads)))[None]
            dt_raw = norm_matmul(xf, mixer_norm_w[i], w_dt, 0, n_out=LANES, tm=1024, tn=LANES,
                                 out_dtype=F32, name=f"ssd_dt_{i}")
            y = ssd_scan(zx, dt_raw, ssd_conv_w[j], ssd_conv_b[j], ssd_dt_bias[j], ssd_a_log[j],
                         ssd_d[j], ssd_norm_w[j], batch=batch, name=f"ssd_scan_{i}")
            xf = matmul_residual(y, ssd_out_w, j, xf, tm=1024, tn=1024, tk=2048,
                                 name=f"ssd_out_{i}")
        else:
            w_qk = _deinterleave_heads(ret_in_w[j, :, :2 * ret_qk], 2 * RET_HEADS).astype(BF16)
            proj = norm_matmul_split(xf, mixer_norm_w[i], w_qk, ret_in_w, j, tm=1024, tn=1024,
                                     out_dtype=BF16, name=f"ret_in_{i}")
            y = retention_scan(proj, batch=batch, qk_dim=ret_qk, v_dim=ret_v,
                               name=f"ret_scan_{i}")
            xf = matmul_residual(y, ret_out_w, j, xf, tm=1024, tn=1024, tk=2048,
                                 name=f"ret_out_{i}")

        q = norm_matmul(xf, mem_q_norm_w[i], mem_q_w, i, n_out=d, tm=1024, tn=1024,
                        out_dtype=BF16, name=f"mem_q_{i}")
        kv = norm_matmul(memf, mem_kv_norm_w[i], mem_kv_w, i, n_out=2 * d, tm=512, tn=1024,
                         out_dtype=BF16, name=f"mem_kv_{i}")
        o = cross_attention(q, kv, batch=batch, heads=MEM_HEADS, tq=512, name=f"mem_attn_{i}")
        xf = matmul_residual(o, mem_o_w, i, xf, tm=1024, tn=1024, tk=2048, name=f"mem_o_{i}")

        hid = norm_glu(xf, ffn_norm_w[i], ffn_in_w, i, tm=1024, tn=512, name=f"ffn_in_{i}")
        xf = matmul_residual(hid, ffn_out_w, i, xf, tm=1024, tn=1024,
                             tk=hid.shape[1] // 4, name=f"ffn_out_{i}")

    out = rms_norm_rows(xf, final_norm_w, tm=512, name="final_norm")
    return out.reshape(batch, seq, d)
```

```python
import functools

import numpy as np
import jax
import jax.numpy as jnp
from jax import lax
from jax.experimental import pallas as pl
from jax.experimental.pallas import tpu as pltpu

NORM_EPS = 1e-6
CHUNK = 128
LANES = 128
SUBLANES = 8
VMEM_LIMIT_BYTES = 56 * 1024 * 1024

SSD_HEAD_DIM = 64
SSD_GROUPS = 8
SSD_STATE = 128
SSD_CONV = 4
RET_HEADS = 8
MEM_HEADS = 4
ROPE_BASE = 10000.0

F32 = jnp.float32
BF16 = jnp.bfloat16


def _params(semantics):
    return pltpu.CompilerParams(dimension_semantics=semantics,
                                vmem_limit_bytes=VMEM_LIMIT_BYTES)


def _silu(v):
    h = 0.5 * v
    return h + h * jnp.tanh(h)


def _rms_rows(x, w):
    ms = jnp.mean(x * x, axis=-1, keepdims=True)
    return x * lax.rsqrt(ms + NORM_EPS) * w


def _norm_prologue(x_ref, nw_ref, h_ref):
    @pl.when(pl.program_id(1) == 0)
    def _():
        h_ref[...] = _rms_rows(x_ref[...], nw_ref[...]).astype(h_ref.dtype)


def _norm_mm_kernel(x_ref, nw_ref, w_ref, o_ref, h_ref):
    _norm_prologue(x_ref, nw_ref, h_ref)
    o_ref[...] = jnp.dot(h_ref[...], w_ref[...],
                         preferred_element_type=F32).astype(o_ref.dtype)


def norm_matmul(x, nw, w, layer, *, n_out, tm, tn, out_dtype, name):
    m, k = x.shape
    tm = min(tm, m)
    return pl.pallas_call(
        _norm_mm_kernel,
        grid=(m // tm, n_out // tn),
        in_specs=[pl.BlockSpec((tm, k), lambda i, j: (i, 0)),
                  pl.BlockSpec((1, k), lambda i, j: (0, 0)),
                  pl.BlockSpec((None, k, tn), lambda i, j: (layer, 0, j))],
        out_specs=pl.BlockSpec((tm, tn), lambda i, j: (i, j)),
        out_shape=jax.ShapeDtypeStruct((m, n_out), out_dtype),
        scratch_shapes=[pltpu.VMEM((tm, k), BF16)],
        compiler_params=_params(("arbitrary", "arbitrary")),
        name=name,
    )(x, nw.reshape(1, k), w)


def _norm_mm_extra_kernel(x_ref, nw_ref, w_ref, we_ref, o_ref, oe_ref, h_ref):
    _norm_prologue(x_ref, nw_ref, h_ref)
    o_ref[...] = jnp.dot(h_ref[...], w_ref[...],
                         preferred_element_type=F32).astype(o_ref.dtype)

    @pl.when(pl.program_id(1) == pl.num_programs(1) - 1)
    def _():
        oe_ref[...] = jnp.dot(h_ref[...], we_ref[...], preferred_element_type=F32)


def norm_matmul_extra(x, nw, w, layer, w_extra, *, n_out, tm, tn, out_dtype, name):
    m, k = x.shape
    ne = w_extra.shape[1]
    tm = min(tm, m)
    return pl.pallas_call(
        _norm_mm_extra_kernel,
        grid=(m // tm, n_out // tn),
        in_specs=[pl.BlockSpec((tm, k), lambda i, j: (i, 0)),
                  pl.BlockSpec((1, k), lambda i, j: (0, 0)),
                  pl.BlockSpec((None, k, tn), lambda i, j: (layer, 0, j)),
                  pl.BlockSpec((k, ne), lambda i, j: (0, 0))],
        out_specs=[pl.BlockSpec((tm, tn), lambda i, j: (i, j)),
                   pl.BlockSpec((tm, ne), lambda i, j: (i, 0))],
        out_shape=[jax.ShapeDtypeStruct((m, n_out), out_dtype),
                   jax.ShapeDtypeStruct((m, ne), F32)],
        scratch_shapes=[pltpu.VMEM((tm, k), BF16)],
        compiler_params=_params(("arbitrary", "arbitrary")),
        name=name,
    )(x, nw.reshape(1, k), w, w_extra)


def _norm_mm_split_kernel(x_ref, nw_ref, wa_ref, wb_ref, o_ref, h_ref, *, na):
    _norm_prologue(x_ref, nw_ref, h_ref)
    j = pl.program_id(1)

    @pl.when(j < na)
    def _():
        o_ref[...] = jnp.dot(h_ref[...], wa_ref[...],
                             preferred_element_type=F32).astype(o_ref.dtype)

    @pl.when(j >= na)
    def _():
        o_ref[...] = jnp.dot(h_ref[...], wb_ref[...],
                             preferred_element_type=F32).astype(o_ref.dtype)


def norm_matmul_split(x, nw, wa, w, layer, *, tm, tn, out_dtype, name):
    m, k = x.shape
    n_out = w.shape[2]
    na = wa.shape[1] // tn
    tm = min(tm, m)
    return pl.pallas_call(
        functools.partial(_norm_mm_split_kernel, na=na),
        grid=(m // tm, n_out // tn),
        in_specs=[pl.BlockSpec((tm, k), lambda i, j: (i, 0)),
                  pl.BlockSpec((1, k), lambda i, j: (0, 0)),
                  pl.BlockSpec((k, tn), lambda i, j: (0, jnp.minimum(j, na - 1))),
                  pl.BlockSpec((None, k, tn), lambda i, j: (layer, 0, jnp.maximum(j, na)))],
        out_specs=pl.BlockSpec((tm, tn), lambda i, j: (i, j)),
        out_shape=jax.ShapeDtypeStruct((m, n_out), out_dtype),
        scratch_shapes=[pltpu.VMEM((tm, k), BF16)],
        compiler_params=_params(("arbitrary", "arbitrary")),
        name=name,
    )(x, nw.reshape(1, k), wa, w)


def _norm_glu_kernel(x_ref, nw_ref, wg_ref, wu_ref, o_ref, h_ref):
    _norm_prologue(x_ref, nw_ref, h_ref)
    h = h_ref[...]
    g = jnp.dot(h, wg_ref[...], preferred_element_type=F32)
    u = jnp.dot(h, wu_ref[...], preferred_element_type=F32)
    o_ref[...] = (_silu(g) * u).astype(o_ref.dtype)


def norm_glu(x, nw, w, layer, *, tm, tn, name):
    m, k = x.shape
    hidden = w.shape[2] // 2
    tm = min(tm, m)
    nj = hidden // tn
    return pl.pallas_call(
        _norm_glu_kernel,
        grid=(m // tm, nj),
        in_specs=[pl.BlockSpec((tm, k), lambda i, j: (i, 0)),
                  pl.BlockSpec((1, k), lambda i, j: (0, 0)),
                  pl.BlockSpec((None, k, tn), lambda i, j: (layer, 0, j)),
                  pl.BlockSpec((None, k, tn), lambda i, j: (layer, 0, j + nj))],
        out_specs=pl.BlockSpec((tm, tn), lambda i, j: (i, j)),
        out_shape=jax.ShapeDtypeStruct((m, hidden), BF16),
        scratch_shapes=[pltpu.VMEM((tm, k), BF16)],
        compiler_params=_params(("arbitrary", "arbitrary")),
        name=name,
    )(x, nw.reshape(1, k), w, w)


def _mm_res_kernel(a_ref, w_ref, r_ref, o_ref, *, nk):
    p = jnp.dot(a_ref[...], w_ref[...], preferred_element_type=F32)
    if nk == 1:
        o_ref[...] = r_ref[...] + p
    else:
        k = pl.program_id(2)

        @pl.when(k == 0)
        def _():
            o_ref[...] = r_ref[...] + p

        @pl.when(k > 0)
        def _():
            o_ref[...] += p


def matmul_residual(a, w, layer, res, *, tm, tn, tk, name):
    m, k = a.shape
    n = w.shape[2]
    tm = min(tm, m)
    nk = k // tk
    return pl.pallas_call(
        functools.partial(_mm_res_kernel, nk=nk),
        grid=(m // tm, n // tn, nk),
        in_specs=[pl.BlockSpec((tm, tk), lambda i, j, kk: (i, kk)),
                  pl.BlockSpec((None, tk, tn), lambda i, j, kk: (layer, kk, j)),
                  pl.BlockSpec((tm, tn), lambda i, j, kk: (i, j))],
        out_specs=pl.BlockSpec((tm, tn), lambda i, j, kk: (i, j)),
        out_shape=jax.ShapeDtypeStruct((m, n), F32),
        compiler_params=_params(("arbitrary", "arbitrary", "arbitrary")),
        name=name,
    )(a, w, res)


def _rms_kernel(x_ref, nw_ref, o_ref):
    o_ref[...] = _rms_rows(x_ref[...], nw_ref[...])


def rms_norm_rows(x, nw, *, tm, name):
    m, k = x.shape
    tm = min(tm, m)
    return pl.pallas_call(
        _rms_kernel,
        grid=(m // tm,),
        in_specs=[pl.BlockSpec((tm, k), lambda i: (i, 0)),
                  pl.BlockSpec((1, k), lambda i: (0, 0))],
        out_specs=pl.BlockSpec((tm, k), lambda i: (i, 0)),
        out_shape=jax.ShapeDtypeStruct((m, k), F32),
        compiler_params=_params(("arbitrary",)),
        name=name,
    )(x, nw.reshape(1, k))


def _mem_block_kernel(x_ref, nw_ref, wq_ref, k_ref, v_ref, wo_ref, o_ref, *, heads, hd):
    scale = hd ** -0.5
    x = x_ref[...]
    h = _rms_rows(x, nw_ref[...]).astype(BF16)
    q = jnp.dot(h, wq_ref[...], preferred_element_type=F32).astype(BF16)
    outs = []
    for hh in range(heads):
        sl = slice(hh * hd, (hh + 1) * hd)
        s = lax.dot_general(q[:, sl], k_ref[:, sl], (((1,), (1,)), ((), ())),
                            preferred_element_type=F32) * scale
        e = jnp.exp(s - jnp.max(s, axis=-1, keepdims=True))
        p = e / jnp.sum(e, axis=-1, keepdims=True)
        outs.append(jnp.dot(p.astype(BF16), v_ref[:, sl],
                            preferred_element_type=F32).astype(BF16))
    o = jnp.concatenate(outs, axis=1)
    o_ref[...] = x + jnp.dot(o, wo_ref[...], preferred_element_type=F32)


def memory_block(x, nw, wq, wo, layer, kv, *, batch, heads, tm, name):
    t, d = x.shape
    seq = t // batch
    mtok = kv.shape[0] // batch
    tm = min(tm, seq)
    per_batch = seq // tm
    resident = lambda: pl.BlockSpec((None, d, d), lambda i: (layer, 0, 0),
                                    pipeline_mode=pl.Buffered(1))
    return pl.pallas_call(
        functools.partial(_mem_block_kernel, heads=heads, hd=d // heads),
        grid=(t // tm,),
        in_specs=[pl.BlockSpec((tm, d), lambda i: (i, 0)),
                  pl.BlockSpec((1, d), lambda i: (0, 0)),
                  resident(),
                  pl.BlockSpec((mtok, d), lambda i: (i // per_batch, 0)),
                  pl.BlockSpec((mtok, d), lambda i: (i // per_batch, 1)),
                  resident()],
        out_specs=pl.BlockSpec((tm, d), lambda i: (i, 0)),
        out_shape=jax.ShapeDtypeStruct((t, d), F32),
        compiler_params=_params(("arbitrary",)),
        name=name,
    )(x, nw.reshape(1, d), wq, kv, kv, wo)


_SPLIT_TERMS = 3


def _split_bf16_terms(v):
    terms = []
    r = v
    for _ in range(_SPLIT_TERMS):
        t = r.astype(BF16).astype(F32)
        terms.append(t)
        r = r - t
    return terms


def _ssd_selector(hpg, hdim):
    sel = np.zeros((LANES, hpg * LANES + hpg * hdim), np.float32)
    for t in range(_SPLIT_TERMS):
        for e in range(hpg):
            sel[t * hpg + e, e * LANES:(e + 1) * LANES] = 1.0
            base = hpg * LANES
            sel[(_SPLIT_TERMS + t) * hpg + e, base + e * hdim:base + (e + 1) * hdim] = 1.0
    return jnp.asarray(sel, dtype=BF16)


def _ssd_kernel(z_ref, x_ref, b_ref, c_ref, dt_ref,
                cwx_ref, cbx_ref, cwb_ref, cbb_ref, cwc_ref, cbc_ref,
                dtb_ref, alog_ref, dsk_ref, nw_ref, sel_ref, o_ref,
                car_x, car_b, car_c, xs_s, bs_s, cs_s, p_s, ea_s, act_s, dtt_s, st_s, y_s,
                *, groups, hpg, hdim, nstate):
    L = CHUNK
    gw = hpg * hdim
    assert 2 * hdim == LANES and nstate == LANES and hpg == SUBLANES

    @pl.when(pl.program_id(1) == 0)
    def _():
        st_s[...] = jnp.zeros_like(st_s)
        car_x[...] = jnp.zeros_like(car_x)
        car_b[...] = jnp.zeros_like(car_b)
        car_c[...] = jnp.zeros_like(car_c)

    def conv_silu(src_ref, w_ref, bias_ref, car_ref, lo, width):
        cols = slice(lo, lo + width)
        u = src_ref[:, cols].astype(F32)
        top = lax.broadcasted_iota(jnp.int32, (SUBLANES, width), 0) == 0
        t = w_ref[0:1, cols] * u
        for k in range(1, SSD_CONV):
            rolled = pltpu.roll(t, 1, 0)
            head = jnp.where(top, car_ref[k - 1:k, cols], rolled[0:SUBLANES])
            car_ref[k - 1:k, cols] = t[L - 1:L]
            t = w_ref[k:k + 1, cols] * u + jnp.concatenate([head, rolled[SUBLANES:]], axis=0)
        return _silu(t + bias_ref[:, cols])

    for g in range(groups):
        for j in range(gw // LANES):
            xs_s[g, :, j * LANES:(j + 1) * LANES] = conv_silu(
                x_ref, cwx_ref, cbx_ref, car_x, g * gw + j * LANES, LANES)
        bs_s[g] = conv_silu(b_ref, cwb_ref, cbb_ref, car_b, g * nstate, nstate).astype(BF16)
        cs_s[g] = conv_silu(c_ref, cwc_ref, cbc_ref, car_c, g * nstate, nstate).astype(BF16)

    dt_in = dt_ref[...] + dtb_ref[...]
    dt = jnp.maximum(dt_in, 0.0) + jnp.log1p(jnp.exp(-jnp.abs(dt_in)))
    dta = dt * (-jnp.exp(alog_ref[...]))
    row = lax.broadcasted_iota(jnp.int32, (L, LANES), 0)
    lane = lax.broadcasted_iota(jnp.int32, (L, LANES), 1)
    acum = dta
    d = 1
    while d < L:
        acum = acum + jnp.where(row >= d, pltpu.roll(acum, d, 0), 0.0)
        d *= 2
    act_s[...] = acum.T
    dtt_s[...] = dt.T
    dtw = dt * jnp.exp(acum[L - 1:L, :] - acum)
    ea = jnp.exp(acum)
    pieces = _split_bf16_terms(acum) + _split_bf16_terms(dtw)
    for g in range(groups):
        packed = jnp.zeros((L, LANES), F32)
        for t, piece in enumerate(pieces):
            shift = (hpg * t - hpg * g) % LANES
            moved = pltpu.roll(piece, shift, 1) if shift else piece
            packed = jnp.where((lane >= hpg * t) & (lane < hpg * (t + 1)), moved, packed)
        p_s[g] = packed.astype(BF16)
        shift = (LANES - hpg * g) % LANES
        ea_s[g] = pltpu.roll(ea, shift, 1) if shift else ea

    first_head = lane < hdim
    causal = (lax.broadcasted_iota(jnp.int32, (L, L), 0)
              >= lax.broadcasted_iota(jnp.int32, (L, L), 1))
    n_acol = hpg * LANES

    def group_body(g, carry):
        xs = xs_s[g]
        bc = bs_s[g]
        cc = cs_s[g]
        eag = ea_s[g]
        rows = pl.ds(pl.multiple_of(g * hpg, hpg), hpg)
        at = act_s[rows, :]
        dtt = dtt_s[rows, :]
        st = st_s[g]
        bcast = jnp.dot(p_s[g], sel_ref[...], preferred_element_type=F32)
        cb = lax.dot_general(cc, bc, (((1,), (1,)), ((), ())), preferred_element_type=F32)
        ccf = cc.astype(F32)
        ys, cds = [], []
        for j in range(hpg // 2):
            tile = slice(j * LANES, (j + 1) * LANES)
            lhs_parts = []
            for e in (2 * j, 2 * j + 1):
                acol = bcast[:, e * LANES:(e + 1) * LANES]
                decay = jnp.exp(jnp.where(causal, acol - at[e:e + 1, :], -jnp.inf))
                lhs_parts.append(jnp.concatenate(
                    [(cb * decay * dtt[e:e + 1, :]).astype(BF16),
                     (ccf * eag[:, e:e + 1]).astype(BF16)], axis=1))
            lhs = jnp.concatenate(lhs_parts, axis=0)
            rhs = jnp.concatenate([xs[:, tile].astype(BF16), st[:, tile].astype(BF16)], axis=0)
            yy = jnp.dot(lhs, rhs, preferred_element_type=F32)
            ys.append(jnp.where(first_head, yy[:L], yy[L:]))
            cds.append(jnp.where(first_head[0:1], eag[L - 1:L, 2 * j:2 * j + 1],
                                 eag[L - 1:L, 2 * j + 1:2 * j + 2]))
        y = jnp.concatenate(ys, axis=1)
        cd = jnp.concatenate(cds, axis=1)
        xdtw = (xs * bcast[:, n_acol:]).astype(BF16)
        contrib = lax.dot_general(bc, xdtw, (((0,), (0,)), ((), ())),
                                  preferred_element_type=F32)
        st_s[g] = st * cd + contrib
        y_s[g] = y + xs * dsk_ref[g]
        return carry

    lax.fori_loop(0, groups, group_body, 0, unroll=4)

    for g in range(groups):
        cols = slice(g * gw, (g + 1) * gw)
        y = y_s[g] * _silu(z_ref[:, cols].astype(F32))
        y = y * lax.rsqrt(jnp.mean(y * y, axis=-1, keepdims=True) + NORM_EPS)
        o_ref[:, cols] = (y * nw_ref[:, cols]).astype(o_ref.dtype)


def ssd_scan(zx, dt_raw, conv_w, conv_b, dt_bias, a_log, d_skip, norm_w, *, batch, name):
    t = zx.shape[0]
    seq = t // batch
    nc = seq // CHUNK
    groups, nstate, hdim = SSD_GROUPS, SSD_STATE, SSD_HEAD_DIM
    d_inner = norm_w.shape[0]
    heads = d_inner // hdim
    hpg = heads // groups
    gw = hpg * hdim
    gn = groups * nstate
    assert d_inner % gn == 0 and zx.shape[1] == 2 * d_inner + 2 * gn
    assert 2 * _SPLIT_TERMS * hpg <= LANES
    nb = d_inner // gn

    cwx, cwb, cwc = conv_w[:, :d_inner], conv_w[:, d_inner:d_inner + gn], conv_w[:, d_inner + gn:]
    cbx, cbb, cbc = (conv_b[None, :d_inner], conv_b[None, d_inner:d_inner + gn],
                     conv_b[None, d_inner + gn:])
    pad = LANES - heads
    dtb = jnp.pad(dt_bias, (0, pad)).reshape(1, LANES)
    alog = jnp.pad(a_log, (0, pad)).reshape(1, LANES)
    dsk = jnp.repeat(d_skip, hdim).reshape(groups, 1, gw)
    nw = norm_w.reshape(1, d_inner)
    sel = _ssd_selector(hpg, hdim)

    row = lambda b, c: b * nc + c
    full = lambda shape: pl.BlockSpec(shape, lambda b, c: (0,) * len(shape))
    return pl.pallas_call(
        functools.partial(_ssd_kernel, groups=groups, hpg=hpg, hdim=hdim, nstate=nstate),
        grid=(batch, nc),
        in_specs=[pl.BlockSpec((CHUNK, d_inner), lambda b, c: (row(b, c), 0)),
                  pl.BlockSpec((CHUNK, d_inner), lambda b, c: (row(b, c), 1)),
                  pl.BlockSpec((CHUNK, gn), lambda b, c: (row(b, c), 2 * nb)),
                  pl.BlockSpec((CHUNK, gn), lambda b, c: (row(b, c), 2 * nb + 1)),
                  pl.BlockSpec((CHUNK, LANES), lambda b, c: (row(b, c), 0)),
                  full((SSD_CONV, d_inner)), full((1, d_inner)),
                  full((SSD_CONV, gn)), full((1, gn)),
                  full((SSD_CONV, gn)), full((1, gn)),
                  full((1, LANES)), full((1, LANES)),
                  full((groups, 1, gw)), full((1, d_inner)), full(sel.shape)],
        out_specs=pl.BlockSpec((CHUNK, d_inner), lambda b, c: (row(b, c), 0)),
        out_shape=jax.ShapeDtypeStruct((t, d_inner), BF16),
        scratch_shapes=[pltpu.VMEM((SUBLANES, d_inner), F32),
                        pltpu.VMEM((SUBLANES, gn), F32),
                        pltpu.VMEM((SUBLANES, gn), F32),
                        pltpu.VMEM((groups, CHUNK, gw), F32),
                        pltpu.VMEM((groups, CHUNK, nstate), BF16),
                        pltpu.VMEM((groups, CHUNK, nstate), BF16),
                        pltpu.VMEM((groups, CHUNK, LANES), BF16),
                        pltpu.VMEM((groups, CHUNK, LANES), F32),
                        pltpu.VMEM((LANES, CHUNK), F32),
                        pltpu.VMEM((LANES, CHUNK), F32),
                        pltpu.VMEM((groups, nstate, gw), F32),
                        pltpu.VMEM((groups, CHUNK, gw), F32)],
        compiler_params=_params(("arbitrary", "arbitrary")),
        name=name,
    )(zx, zx, zx, zx, dt_raw, cwx, cbx, cwb, cbb, cwc, cbc, dtb, alog, dsk, nw, sel)


def _ret_kernel(q_ref, k_ref, v_ref, g_ref, cos_ref, sin_ref,
                idec_ref, qdec_ref, kdec_ref, cdec_ref, o_ref, st_s, *, heads, dk, dv):
    @pl.when(pl.program_id(1) == 0)
    def _():
        st_s[...] = jnp.zeros_like(st_s)

    cos = cos_ref[...]
    sin = sin_ref[...]
    half = dk // 2

    def rope(ref, h):
        a = ref[:, h * dk:h * dk + half].astype(F32)
        b = ref[:, h * dk + half:(h + 1) * dk].astype(F32)
        return jnp.concatenate([a * cos - b * sin, b * cos + a * sin], axis=1)

    for h in range(heads):
        qr = rope(q_ref, h)
        kr = rope(k_ref, h) * dk ** -0.5
        v = v_ref[:, h * dv:(h + 1) * dv]
        s = lax.dot_general(qr.astype(BF16), kr.astype(BF16), (((1,), (1,)), ((), ())),
                            preferred_element_type=F32) * idec_ref[h]
        intra = jnp.dot(s.astype(BF16), v, preferred_element_type=F32)
        st = st_s[h]
        cross = jnp.dot((qr * qdec_ref[h]).astype(BF16), st.astype(BF16),
                        preferred_element_type=F32)
        st_s[h] = st * cdec_ref[h] + lax.dot_general(
            (kr * kdec_ref[h]).astype(BF16), v, (((0,), (0,)), ((), ())),
            preferred_element_type=F32)
        y = intra + cross
        y = y * lax.rsqrt(jnp.mean(y * y, axis=-1, keepdims=True) + NORM_EPS)
        gate = g_ref[:, h * dv:(h + 1) * dv].astype(F32)
        o_ref[:, h * dv:(h + 1) * dv] = (y * _silu(gate)).astype(o_ref.dtype)


def retention_scan(proj, *, batch, qk_dim, v_dim, name):
    t = proj.shape[0]
    seq = t // batch
    nc = seq // CHUNK
    heads = RET_HEADS
    dk, dv = qk_dim // heads, v_dim // heads
    assert v_dim == 2 * qk_dim and proj.shape[1] == 2 * qk_dim + 2 * v_dim

    pos = jnp.arange(seq, dtype=F32)
    inv_freq = 1.0 / (ROPE_BASE ** jnp.linspace(0.0, 1.0, dk // 2, dtype=F32))
    theta = pos[:, None] * inv_freq[None, :]
    cos, sin = jnp.cos(theta), jnp.sin(theta)
    log_gamma = jnp.log1p(-jnp.exp2(-5.0 - jnp.arange(heads, dtype=F32)))
    idx = jnp.arange(CHUNK, dtype=F32)
    causal = jnp.tril(jnp.ones((CHUNK, CHUNK), dtype=bool))
    idec = jnp.exp(jnp.where(causal[None],
                             (idx[:, None] - idx[None, :])[None] * log_gamma[:, None, None],
                             -jnp.inf))
    qdec = jnp.exp(log_gamma[:, None] * (idx + 1.0)[None, :])[..., None]
    kdec = jnp.exp(log_gamma[:, None] * (CHUNK - 1.0 - idx)[None, :])[..., None]
    cdec = jnp.exp(CHUNK * log_gamma)[:, None, None]

    row = lambda b, c: b * nc + c
    full = lambda shape: pl.BlockSpec(shape, lambda b, c: (0,) * len(shape))
    return pl.pallas_call(
        functools.partial(_ret_kernel, heads=heads, dk=dk, dv=dv),
        grid=(batch, nc),
        in_specs=[pl.BlockSpec((CHUNK, qk_dim), lambda b, c: (row(b, c), 0)),
                  pl.BlockSpec((CHUNK, qk_dim), lambda b, c: (row(b, c), 1)),
                  pl.BlockSpec((CHUNK, v_dim), lambda b, c: (row(b, c), 1)),
                  pl.BlockSpec((CHUNK, v_dim), lambda b, c: (row(b, c), 2)),
                  pl.BlockSpec((CHUNK, dk // 2), lambda b, c: (c, 0)),
                  pl.BlockSpec((CHUNK, dk // 2), lambda b, c: (c, 0)),
                  full((heads, CHUNK, CHUNK)), full((heads, CHUNK, 1)),
                  full((heads, CHUNK, 1)), full((heads, 1, 1))],
        out_specs=pl.BlockSpec((CHUNK, v_dim), lambda b, c: (row(b, c), 0)),
        out_shape=jax.ShapeDtypeStruct((t, v_dim), BF16),
        scratch_shapes=[pltpu.VMEM((heads, dk, dv), F32)],
        compiler_params=_params(("arbitrary", "arbitrary")),
        name=name,
    )(proj, proj, proj, proj, cos, sin, idec, qdec, kdec, cdec)


def _deinterleave_heads(w, heads):
    k, n = w.shape
    return w.reshape(k, heads, n // heads // 2, 2).transpose(0, 1, 3, 2).reshape(k, n)


def kernel(x, mem, ssd_in_w, ssd_conv_w, ssd_conv_b, ssd_dt_bias, ssd_a_log, ssd_d, ssd_norm_w,
           ssd_out_w, ret_in_w, ret_out_w, mixer_norm_w, mem_q_norm_w, mem_kv_norm_w, mem_q_w,
           mem_kv_w, mem_o_w, ffn_norm_w, ffn_in_w, ffn_out_w, final_norm_w):
    batch, seq, d = x.shape
    depth = mixer_norm_w.shape[0]
    xf = x.reshape(batch * seq, d)
    memf = mem.reshape(-1, d)
    ssd_inner = ssd_norm_w.shape[1]
    ssd_main = 2 * ssd_inner + 2 * SSD_GROUPS * SSD_STATE
    ssd_heads = ssd_inner // SSD_HEAD_DIM
    ret_qk = ret_in_w.shape[2] // 6
    ret_v = 2 * ret_qk

    ssd_in_b, ssd_out_b = ssd_in_w.astype(BF16), ssd_out_w.astype(BF16)
    ret_in_b, ret_out_b = ret_in_w.astype(BF16), ret_out_w.astype(BF16)
    mem_q_b, mem_kv_b, mem_o_b = mem_q_w.astype(BF16), mem_kv_w.astype(BF16), mem_o_w.astype(BF16)
    ffn_in_b, ffn_out_b = ffn_in_w.astype(BF16), ffn_out_w.astype(BF16)

    for i in range(depth):
        j = i // 2
        if i % 2 == 0:
            w_dt = jnp.pad(ssd_in_b[j, :, ssd_main:], ((0, 0), (0, LANES - ssd_heads)))
            zx, dt_raw = norm_matmul_extra(xf, mixer_norm_w[i], ssd_in_b, j, w_dt, n_out=ssd_main,
                                           tm=1024, tn=1024, out_dtype=BF16, name=f"ssd_in_{i}")
            y = ssd_scan(zx, dt_raw, ssd_conv_w[j], ssd_conv_b[j], ssd_dt_bias[j], ssd_a_log[j],
                         ssd_d[j], ssd_norm_w[j], batch=batch, name=f"ssd_scan_{i}")
            xf = matmul_residual(y, ssd_out_b, j, xf, tm=1024, tn=1024, tk=2048,
                                 name=f"ssd_out_{i}")
        else:
            w_qk = _deinterleave_heads(ret_in_w[j, :, :2 * ret_qk], 2 * RET_HEADS).astype(BF16)
            proj = norm_matmul_split(xf, mixer_norm_w[i], w_qk, ret_in_b, j, tm=1024, tn=1024,
                                     out_dtype=BF16, name=f"ret_in_{i}")
            y = retention_scan(proj, batch=batch, qk_dim=ret_qk, v_dim=ret_v,
                               name=f"ret_scan_{i}")
            xf = matmul_residual(y, ret_out_b, j, xf, tm=1024, tn=1024, tk=2048,
                                 name=f"ret_out_{i}")

        kv = norm_matmul(memf, mem_kv_norm_w[i], mem_kv_b, i, n_out=2 * d, tm=512, tn=1024,
                         out_dtype=BF16, name=f"mem_kv_{i}")
        xf = memory_block(xf, mem_q_norm_w[i], mem_q_b, mem_o_b, i, kv, batch=batch,
                          heads=MEM_HEADS, tm=512, name=f"mem_blk_{i}")

        hid = norm_glu(xf, ffn_norm_w[i], ffn_in_b, i, tm=1024, tn=512, name=f"ffn_in_{i}")
        xf = matmul_residual(hid, ffn_out_b, i, xf, tm=1024, tn=1024,
                             tk=hid.shape[1] // 2, name=f"ffn_out_{i}")

    out = rms_norm_rows(xf, final_norm_w, tm=512, name="final_norm")
    return out.reshape(batch, seq, d)
```

```python
import functools

import numpy as np
import jax
import jax.numpy as jnp
from jax import lax
from jax.experimental import pallas as pl
from jax.experimental.pallas import tpu as pltpu

NORM_EPS = 1e-6
CHUNK = 128
LANES = 128
SUBLANES = 8
VMEM_LIMIT_BYTES = 56 * 1024 * 1024

SSD_HEAD_DIM = 64
SSD_GROUPS = 8
SSD_STATE = 128
SSD_CONV = 4
RET_HEADS = 8
MEM_HEADS = 4
ROPE_BASE = 10000.0

F32 = jnp.float32
BF16 = jnp.bfloat16


def _params(semantics):
    return pltpu.CompilerParams(dimension_semantics=semantics,
                                vmem_limit_bytes=VMEM_LIMIT_BYTES)


def _silu(v):
    h = 0.5 * v
    return h + h * jnp.tanh(h)


def _rms_rows(x, w):
    ms = jnp.mean(x * x, axis=-1, keepdims=True)
    return x * lax.rsqrt(ms + NORM_EPS) * w


def _norm_prologue(x_ref, nw_ref, h_ref):
    @pl.when(pl.program_id(1) == 0)
    def _():
        h_ref[...] = _rms_rows(x_ref[...], nw_ref[...]).astype(h_ref.dtype)


def _norm_mm_kernel(x_ref, nw_ref, w_ref, o_ref, h_ref):
    _norm_prologue(x_ref, nw_ref, h_ref)
    o_ref[...] = jnp.dot(h_ref[...], w_ref[...],
                         preferred_element_type=F32).astype(o_ref.dtype)


def norm_matmul(x, nw, w, layer, *, n_out, tm, tn, out_dtype, name):
    m, k = x.shape
    tm = min(tm, m)
    return pl.pallas_call(
        _norm_mm_kernel,
        grid=(m // tm, n_out // tn),
        in_specs=[pl.BlockSpec((tm, k), lambda i, j: (i, 0)),
                  pl.BlockSpec((1, k), lambda i, j: (0, 0)),
                  pl.BlockSpec((None, k, tn), lambda i, j: (layer, 0, j))],
        out_specs=pl.BlockSpec((tm, tn), lambda i, j: (i, j)),
        out_shape=jax.ShapeDtypeStruct((m, n_out), out_dtype),
        scratch_shapes=[pltpu.VMEM((tm, k), BF16)],
        compiler_params=_params(("arbitrary", "arbitrary")),
        name=name,
    )(x, nw.reshape(1, k), w)


def _norm_mm_extra_kernel(x_ref, nw_ref, w_ref, we_ref, o_ref, oe_ref, h_ref):
    _norm_prologue(x_ref, nw_ref, h_ref)
    o_ref[...] = jnp.dot(h_ref[...], w_ref[...],
                         preferred_element_type=F32).astype(o_ref.dtype)

    @pl.when(pl.program_id(1) == pl.num_programs(1) - 1)
    def _():
        oe_ref[...] = jnp.dot(h_ref[...], we_ref[...], preferred_element_type=F32)


def norm_matmul_extra(x, nw, w, layer, w_extra, *, n_out, tm, tn, out_dtype, name):
    m, k = x.shape
    ne = w_extra.shape[1]
    tm = min(tm, m)
    return pl.pallas_call(
        _norm_mm_extra_kernel,
        grid=(m // tm, n_out // tn),
        in_specs=[pl.BlockSpec((tm, k), lambda i, j: (i, 0)),
                  pl.BlockSpec((1, k), lambda i, j: (0, 0)),
                  pl.BlockSpec((None, k, tn), lambda i, j: (layer, 0, j)),
                  pl.BlockSpec((k, ne), lambda i, j: (0, 0))],
        out_specs=[pl.BlockSpec((tm, tn), lambda i, j: (i, j)),
                   pl.BlockSpec((tm, ne), lambda i, j: (i, 0))],
        out_shape=[jax.ShapeDtypeStruct((m, n_out), out_dtype),
                   jax.ShapeDtypeStruct((m, ne), F32)],
        scratch_shapes=[pltpu.VMEM((tm, k), BF16)],
        compiler_params=_params(("arbitrary", "arbitrary")),
        name=name,
    )(x, nw.reshape(1, k), w, w_extra)


def _norm_glu_kernel(x_ref, nw_ref, wg_ref, wu_ref, o_ref, h_ref):
    _norm_prologue(x_ref, nw_ref, h_ref)
    h = h_ref[...]
    g = jnp.dot(h, wg_ref[...], preferred_element_type=F32)
    u = jnp.dot(h, wu_ref[...], preferred_element_type=F32)
    o_ref[...] = (_silu(g) * u).astype(o_ref.dtype)


def norm_glu(x, nw, w, layer, *, tm, tn, name):
    m, k = x.shape
    hidden = w.shape[2] // 2
    tm = min(tm, m)
    nj = hidden // tn
    return pl.pallas_call(
        _norm_glu_kernel,
        grid=(m // tm, nj),
        in_specs=[pl.BlockSpec((tm, k), lambda i, j: (i, 0)),
                  pl.BlockSpec((1, k), lambda i, j: (0, 0)),
                  pl.BlockSpec((None, k, tn), lambda i, j: (layer, 0, j)),
                  pl.BlockSpec((None, k, tn), lambda i, j: (layer, 0, j + nj))],
        out_specs=pl.BlockSpec((tm, tn), lambda i, j: (i, j)),
        out_shape=jax.ShapeDtypeStruct((m, hidden), BF16),
        scratch_shapes=[pltpu.VMEM((tm, k), BF16)],
        compiler_params=_params(("arbitrary", "arbitrary")),
        name=name,
    )(x, nw.reshape(1, k), w, w)


def _mm_res_kernel(a_ref, w_ref, r_ref, o_ref, *, nk):
    p = jnp.dot(a_ref[...], w_ref[...], preferred_element_type=F32)
    if nk == 1:
        o_ref[...] = r_ref[...] + p
    else:
        k = pl.program_id(2)

        @pl.when(k == 0)
        def _():
            o_ref[...] = r_ref[...] + p

        @pl.when(k > 0)
        def _():
            o_ref[...] += p


def matmul_residual(a, w, layer, res, *, tm, tn, tk, name):
    m, k = a.shape
    n = w.shape[2]
    tm = min(tm, m)
    nk = k // tk
    return pl.pallas_call(
        functools.partial(_mm_res_kernel, nk=nk),
        grid=(m // tm, n // tn, nk),
        in_specs=[pl.BlockSpec((tm, tk), lambda i, j, kk: (i, kk)),
                  pl.BlockSpec((None, tk, tn), lambda i, j, kk: (layer, kk, j)),
                  pl.BlockSpec((tm, tn), lambda i, j, kk: (i, j))],
        out_specs=pl.BlockSpec((tm, tn), lambda i, j, kk: (i, j)),
        out_shape=jax.ShapeDtypeStruct((m, n), F32),
        compiler_params=_params(("arbitrary", "arbitrary", "arbitrary")),
        name=name,
    )(a, w, res)


def _rms_kernel(x_ref, nw_ref, o_ref):
    o_ref[...] = _rms_rows(x_ref[...], nw_ref[...])


def rms_norm_rows(x, nw, *, tm, name):
    m, k = x.shape
    tm = min(tm, m)
    return pl.pallas_call(
        _rms_kernel,
        grid=(m // tm,),
        in_specs=[pl.BlockSpec((tm, k), lambda i: (i, 0)),
                  pl.BlockSpec((1, k), lambda i: (0, 0))],
        out_specs=pl.BlockSpec((tm, k), lambda i: (i, 0)),
        out_shape=jax.ShapeDtypeStruct((m, k), F32),
        compiler_params=_params(("arbitrary",)),
        name=name,
    )(x, nw.reshape(1, k))


def _mem_block_kernel(x_ref, nw_ref, wq_ref, k_ref, v_ref, wo_ref, o_ref, *, heads, hd):
    scale = hd ** -0.5
    x = x_ref[...]
    h = _rms_rows(x, nw_ref[...]).astype(BF16)
    q = jnp.dot(h, wq_ref[...], preferred_element_type=F32).astype(BF16)
    outs = []
    for hh in range(heads):
        sl = slice(hh * hd, (hh + 1) * hd)
        s = lax.dot_general(q[:, sl], k_ref[:, sl], (((1,), (1,)), ((), ())),
                            preferred_element_type=F32) * scale
        e = jnp.exp(s - jnp.max(s, axis=-1, keepdims=True))
        p = e / jnp.sum(e, axis=-1, keepdims=True)
        outs.append(jnp.dot(p.astype(BF16), v_ref[:, sl],
                            preferred_element_type=F32).astype(BF16))
    o = jnp.concatenate(outs, axis=1)
    o_ref[...] = x + jnp.dot(o, wo_ref[...], preferred_element_type=F32)


def memory_block(x, nw, wq, wo, layer, kv, *, batch, heads, tm, name):
    t, d = x.shape
    seq = t // batch
    mtok = kv.shape[0] // batch
    tm = min(tm, seq)
    per_batch = seq // tm
    resident = lambda: pl.BlockSpec((None, d, d), lambda i: (layer, 0, 0),
                                    pipeline_mode=pl.Buffered(1))
    return pl.pallas_call(
        functools.partial(_mem_block_kernel, heads=heads, hd=d // heads),
        grid=(t // tm,),
        in_specs=[pl.BlockSpec((tm, d), lambda i: (i, 0)),
                  pl.BlockSpec((1, d), lambda i: (0, 0)),
                  resident(),
                  pl.BlockSpec((mtok, d), lambda i: (i // per_batch, 0)),
                  pl.BlockSpec((mtok, d), lambda i: (i // per_batch, 1)),
                  resident()],
        out_specs=pl.BlockSpec((tm, d), lambda i: (i, 0)),
        out_shape=jax.ShapeDtypeStruct((t, d), F32),
        compiler_params=_params(("arbitrary",)),
        name=name,
    )(x, nw.reshape(1, d), wq, kv, kv, wo)


_SPLIT_TERMS = 3


def _split_bf16_terms(v):
    terms = []
    r = v
    for _ in range(_SPLIT_TERMS):
        t = r.astype(BF16).astype(F32)
        terms.append(t)
        r = r - t
    return terms


def _ssd_selector(hpg, hdim):
    sel = np.zeros((LANES, hpg * LANES + hpg * hdim), np.float32)
    for t in range(_SPLIT_TERMS):
        for e in range(hpg):
            sel[t * hpg + e, e * LANES:(e + 1) * LANES] = 1.0
            base = hpg * LANES
            sel[(_SPLIT_TERMS + t) * hpg + e, base + e * hdim:base + (e + 1) * hdim] = 1.0
    return jnp.asarray(sel, dtype=BF16)


def _conv_shift_matrix():
    m = np.zeros((SSD_CONV * CHUNK, 2 * CHUNK), np.float32)
    for k in range(SSD_CONV):
        for t in range(CHUNK):
            m[k * CHUNK + t, CHUNK + t - (SSD_CONV - 1) + k] = 1.0
    return jnp.asarray(m, dtype=BF16)


def _ssd_kernel(z_ref, x_ref, b_ref, c_ref, dt_ref,
                cwx_ref, cbx_ref, cwb_ref, cbb_ref, cwc_ref, cbc_ref,
                dtb_ref, alog_ref, dsk_ref, nw_ref, sel_ref, shf_ref, o_ref,
                prev_x, prev_b, prev_c, xs_s, bs_s, cs_s, p_s, ea_s, act_s, dtt_s, st_s, y_s,
                *, groups, hpg, hdim, nstate):
    L = CHUNK
    gw = hpg * hdim
    assert 2 * hdim == LANES and nstate == LANES and hpg == SUBLANES

    @pl.when(pl.program_id(1) == 0)
    def _():
        st_s[...] = jnp.zeros_like(st_s)
        prev_x[...] = jnp.zeros_like(prev_x)
        prev_b[...] = jnp.zeros_like(prev_b)
        prev_c[...] = jnp.zeros_like(prev_c)

    def conv_silu(src_ref, w_ref, bias_ref, prev_ref, lo, width):
        cols = slice(lo, lo + width)
        u = src_ref[:, cols]
        taps = jnp.dot(shf_ref[...], jnp.concatenate([prev_ref[:, cols], u], axis=0),
                       preferred_element_type=F32)
        prev_ref[:, cols] = u
        acc = bias_ref[:, cols] + w_ref[0:1, cols] * taps[0:L]
        for k in range(1, SSD_CONV):
            acc = acc + w_ref[k:k + 1, cols] * taps[k * L:(k + 1) * L]
        return _silu(acc)

    for g in range(groups):
        for j in range(gw // (2 * LANES)):
            cols = slice(j * 2 * LANES, (j + 1) * 2 * LANES)
            xs_s[g, :, cols] = conv_silu(x_ref, cwx_ref, cbx_ref, prev_x,
                                         g * gw + j * 2 * LANES, 2 * LANES)
    for g in range(groups // 2):
        for ref, w_ref, bias_ref, prev_ref, dst in ((b_ref, cwb_ref, cbb_ref, prev_b, bs_s),
                                                    (c_ref, cwc_ref, cbc_ref, prev_c, cs_s)):
            pair = conv_silu(ref, w_ref, bias_ref, prev_ref, 2 * g * nstate, 2 * nstate)
            dst[2 * g] = pair[:, :nstate].astype(BF16)
            dst[2 * g + 1] = pair[:, nstate:].astype(BF16)

    dt_in = dt_ref[...] + dtb_ref[...]
    dt = jnp.maximum(dt_in, 0.0) + jnp.log1p(jnp.exp(-jnp.abs(dt_in)))
    dta = dt * (-jnp.exp(alog_ref[...]))
    row = lax.broadcasted_iota(jnp.int32, (L, LANES), 0)
    lane = lax.broadcasted_iota(jnp.int32, (L, LANES), 1)
    acum = dta
    d = 1
    while d < L:
        acum = acum + jnp.where(row >= d, pltpu.roll(acum, d, 0), 0.0)
        d *= 2
    act_s[...] = acum.T
    dtt_s[...] = dt.T
    dtw = dt * jnp.exp(acum[L - 1:L, :] - acum)
    ea = jnp.exp(acum)
    pieces = _split_bf16_terms(acum) + _split_bf16_terms(dtw)
    for g in range(groups):
        packed = jnp.zeros((L, LANES), F32)
        for t, piece in enumerate(pieces):
            shift = (hpg * t - hpg * g) % LANES
            moved = pltpu.roll(piece, shift, 1) if shift else piece
            packed = jnp.where((lane >= hpg * t) & (lane < hpg * (t + 1)), moved, packed)
        p_s[g] = packed.astype(BF16)
        shift = (LANES - hpg * g) % LANES
        ea_s[g] = pltpu.roll(ea, shift, 1) if shift else ea

    first_head = lane < hdim
    causal = (lax.broadcasted_iota(jnp.int32, (L, L), 0)
              >= lax.broadcasted_iota(jnp.int32, (L, L), 1))
    n_acol = hpg * LANES

    def group_body(g, carry):
        xs = xs_s[g]
        bc = bs_s[g]
        cc = cs_s[g]
        eag = ea_s[g]
        rows = pl.ds(pl.multiple_of(g * hpg, hpg), hpg)
        at = act_s[rows, :]
        dtt = dtt_s[rows, :]
        st = st_s[g]
        bcast = jnp.dot(p_s[g], sel_ref[...], preferred_element_type=F32)
        cb = lax.dot_general(cc, bc, (((1,), (1,)), ((), ())), preferred_element_type=F32)
        ccf = cc.astype(F32)
        ys, cds = [], []
        for j in range(hpg // 2):
            tile = slice(j * LANES, (j + 1) * LANES)
            lhs_parts = []
            for e in (2 * j, 2 * j + 1):
                acol = bcast[:, e * LANES:(e + 1) * LANES]
                decay = jnp.exp(jnp.where(causal, acol - at[e:e + 1, :], -jnp.inf))
                lhs_parts.append(jnp.concatenate(
                    [(cb * decay * dtt[e:e + 1, :]).astype(BF16),
                     (ccf * eag[:, e:e + 1]).astype(BF16)], axis=1))
            lhs = jnp.concatenate(lhs_parts, axis=0)
            rhs = jnp.concatenate([xs[:, tile].astype(BF16), st[:, tile].astype(BF16)], axis=0)
            yy = jnp.dot(lhs, rhs, preferred_element_type=F32)
            ys.append(jnp.where(first_head, yy[:L], yy[L:]))
            cds.append(jnp.where(first_head[0:1], eag[L - 1:L, 2 * j:2 * j + 1],
                                 eag[L - 1:L, 2 * j + 1:2 * j + 2]))
        y = jnp.concatenate(ys, axis=1)
        cd = jnp.concatenate(cds, axis=1)
        xdtw = (xs * bcast[:, n_acol:]).astype(BF16)
        contrib = lax.dot_general(bc, xdtw, (((0,), (0,)), ((), ())),
                                  preferred_element_type=F32)
        st_s[g] = st * cd + contrib
        y_s[g] = y + xs * dsk_ref[g]
        return carry

    lax.fori_loop(0, groups, group_body, 0, unroll=4)

    for g in range(groups):
        cols = slice(g * gw, (g + 1) * gw)
        y = y_s[g] * _silu(z_ref[:, cols].astype(F32))
        y = y * lax.rsqrt(jnp.mean(y * y, axis=-1, keepdims=True) + NORM_EPS)
        o_ref[:, cols] = (y * nw_ref[:, cols]).astype(o_ref.dtype)


def ssd_scan(zx, dt_raw, conv_w, conv_b, dt_bias, a_log, d_skip, norm_w, *, batch, name):
    t = zx.shape[0]
    seq = t // batch
    nc = seq // CHUNK
    groups, nstate, hdim = SSD_GROUPS, SSD_STATE, SSD_HEAD_DIM
    d_inner = norm_w.shape[0]
    heads = d_inner // hdim
    hpg = heads // groups
    gw = hpg * hdim
    gn = groups * nstate
    assert d_inner % gn == 0 and zx.shape[1] == 2 * d_inner + 2 * gn
    assert 2 * _SPLIT_TERMS * hpg <= LANES
    nb = d_inner // gn

    cwx, cwb, cwc = conv_w[:, :d_inner], conv_w[:, d_inner:d_inner + gn], conv_w[:, d_inner + gn:]
    cbx, cbb, cbc = (conv_b[None, :d_inner], conv_b[None, d_inner:d_inner + gn],
                     conv_b[None, d_inner + gn:])
    pad = LANES - heads
    dtb = jnp.pad(dt_bias, (0, pad)).reshape(1, LANES)
    alog = jnp.pad(a_log, (0, pad)).reshape(1, LANES)
    dsk = jnp.repeat(d_skip, hdim).reshape(groups, 1, gw)
    nw = norm_w.reshape(1, d_inner)
    sel = _ssd_selector(hpg, hdim)
    shf = _conv_shift_matrix()

    row = lambda b, c: b * nc + c
    full = lambda shape: pl.BlockSpec(shape, lambda b, c: (0,) * len(shape))
    return pl.pallas_call(
        functools.partial(_ssd_kernel, groups=groups, hpg=hpg, hdim=hdim, nstate=nstate),
        grid=(batch, nc),
        in_specs=[pl.BlockSpec((CHUNK, d_inner), lambda b, c: (row(b, c), 0)),
                  pl.BlockSpec((CHUNK, d_inner), lambda b, c: (row(b, c), 1)),
                  pl.BlockSpec((CHUNK, gn), lambda b, c: (row(b, c), 2 * nb)),
                  pl.BlockSpec((CHUNK, gn), lambda b, c: (row(b, c), 2 * nb + 1)),
                  pl.BlockSpec((CHUNK, LANES), lambda b, c: (row(b, c), 0)),
                  full((SSD_CONV, d_inner)), full((1, d_inner)),
                  full((SSD_CONV, gn)), full((1, gn)),
                  full((SSD_CONV, gn)), full((1, gn)),
                  full((1, LANES)), full((1, LANES)),
                  full((groups, 1, gw)), full((1, d_inner)), full(sel.shape),
                  full(shf.shape)],
        out_specs=pl.BlockSpec((CHUNK, d_inner), lambda b, c: (row(b, c), 0)),
        out_shape=jax.ShapeDtypeStruct((t, d_inner), BF16),
        scratch_shapes=[pltpu.VMEM((CHUNK, d_inner), BF16),
                        pltpu.VMEM((CHUNK, gn), BF16),
                        pltpu.VMEM((CHUNK, gn), BF16),
                        pltpu.VMEM((groups, CHUNK, gw), F32),
                        pltpu.VMEM((groups, CHUNK, nstate), BF16),
                        pltpu.VMEM((groups, CHUNK, nstate), BF16),
                        pltpu.VMEM((groups, CHUNK, LANES), BF16),
                        pltpu.VMEM((groups, CHUNK, LANES), F32),
                        pltpu.VMEM((LANES, CHUNK), F32),
                        pltpu.VMEM((LANES, CHUNK), F32),
                        pltpu.VMEM((groups, nstate, gw), F32),
                        pltpu.VMEM((groups, CHUNK, gw), F32)],
        compiler_params=_params(("arbitrary", "arbitrary")),
        name=name,
    )(zx, zx, zx, zx, dt_raw, cwx, cbx, cwb, cbb, cwc, cbc, dtb, alog, dsk, nw, sel, shf)


def _ret_kernel(q_ref, k_ref, v_ref, g_ref, cos_ref, sin_ref,
                idec_ref, qdec_ref, kdec_ref, cdec_ref, o_ref, st_s, *, heads, dk, dv):
    @pl.when(pl.program_id(1) == 0)
    def _():
        st_s[...] = jnp.zeros_like(st_s)

    cos = cos_ref[...]
    sin = sin_ref[...]
    half = dk // 2

    def rope(ref, h):
        a = ref[:, h * dk:h * dk + half].astype(F32)
        b = ref[:, h * dk + half:(h + 1) * dk].astype(F32)
        return jnp.concatenate([a * cos - b * sin, b * cos + a * sin], axis=1)

    for h in range(heads):
        qr = rope(q_ref, h)
        kr = rope(k_ref, h) * dk ** -0.5
        v = v_ref[:, h * dv:(h + 1) * dv]
        s = lax.dot_general(qr.astype(BF16), kr.astype(BF16), (((1,), (1,)), ((), ())),
                            preferred_element_type=F32) * idec_ref[h]
        intra = jnp.dot(s.astype(BF16), v, preferred_element_type=F32)
        st = st_s[h]
        cross = jnp.dot((qr * qdec_ref[h]).astype(BF16), st.astype(BF16),
                        preferred_element_type=F32)
        st_s[h] = st * cdec_ref[h] + lax.dot_general(
            (kr * kdec_ref[h]).astype(BF16), v, (((0,), (0,)), ((), ())),
            preferred_element_type=F32)
        y = intra + cross
        y = y * lax.rsqrt(jnp.mean(y * y, axis=-1, keepdims=True) + NORM_EPS)
        gate = g_ref[:, h * dv:(h + 1) * dv].astype(F32)
        o_ref[:, h * dv:(h + 1) * dv] = (y * _silu(gate)).astype(o_ref.dtype)


def retention_scan(proj, *, batch, qk_dim, v_dim, name):
    t = proj.shape[0]
    seq = t // batch
    nc = seq // CHUNK
    heads = RET_HEADS
    dk, dv = qk_dim // heads, v_dim // heads
    assert v_dim == 2 * qk_dim and proj.shape[1] == 2 * qk_dim + 2 * v_dim

    pos = jnp.arange(seq, dtype=F32)
    inv_freq = 1.0 / (ROPE_BASE ** jnp.linspace(0.0, 1.0, dk // 2, dtype=F32))
    theta = pos[:, None] * inv_freq[None, :]
    cos, sin = jnp.cos(theta), jnp.sin(theta)
    log_gamma = jnp.log1p(-jnp.exp2(-5.0 - jnp.arange(heads, dtype=F32)))
    idx = jnp.arange(CHUNK, dtype=F32)
    causal = jnp.tril(jnp.ones((CHUNK, CHUNK), dtype=bool))
    idec = jnp.exp(jnp.where(causal[None],
                             (idx[:, None] - idx[None, :])[None] * log_gamma[:, None, None],
                             -jnp.inf))
    qdec = jnp.exp(log_gamma[:, None] * (idx + 1.0)[None, :])[..., None]
    kdec = jnp.exp(log_gamma[:, None] * (CHUNK - 1.0 - idx)[None, :])[..., None]
    cdec = jnp.exp(CHUNK * log_gamma)[:, None, None]

    row = lambda b, c: b * nc + c
    full = lambda shape: pl.BlockSpec(shape, lambda b, c: (0,) * len(shape))
    return pl.pallas_call(
        functools.partial(_ret_kernel, heads=heads, dk=dk, dv=dv),
        grid=(batch, nc),
        in_specs=[pl.BlockSpec((CHUNK, qk_dim), lambda b, c: (row(b, c), 0)),
                  pl.BlockSpec((CHUNK, qk_dim), lambda b, c: (row(b, c), 1)),
                  pl.BlockSpec((CHUNK, v_dim), lambda b, c: (row(b, c), 1)),
                  pl.BlockSpec((CHUNK, v_dim), lambda b, c: (row(b, c), 2)),
                  pl.BlockSpec((CHUNK, dk // 2), lambda b, c: (c, 0)),
                  pl.BlockSpec((CHUNK, dk // 2), lambda b, c: (c, 0)),
                  full((heads, CHUNK, CHUNK)), full((heads, CHUNK, 1)),
                  full((heads, CHUNK, 1)), full((heads, 1, 1))],
        out_specs=pl.BlockSpec((CHUNK, v_dim), lambda b, c: (row(b, c), 0)),
        out_shape=jax.ShapeDtypeStruct((t, v_dim), BF16),
        scratch_shapes=[pltpu.VMEM((heads, dk, dv), F32)],
        compiler_params=_params(("arbitrary", "arbitrary")),
        name=name,
    )(proj, proj, proj, proj, cos, sin, idec, qdec, kdec, cdec)


def _deinterleave_kernel(w_ref, perm_ref, o_ref, *, n_perm):
    @pl.when(pl.program_id(0) < n_perm)
    def _():
        o_ref[...] = jnp.dot(w_ref[...], perm_ref[...],
                             preferred_element_type=F32).astype(o_ref.dtype)

    @pl.when(pl.program_id(0) >= n_perm)
    def _():
        o_ref[...] = w_ref[...]


def deinterleave_qk_columns(w, layer, *, qk_cols, head_dim, name):
    k, n = w.shape[1], w.shape[2]
    src = np.arange(head_dim).reshape(head_dim // 2, 2).T.reshape(-1)
    perm = jnp.asarray(np.eye(head_dim, dtype=np.float32)[:, src], dtype=BF16)
    return pl.pallas_call(
        functools.partial(_deinterleave_kernel, n_perm=qk_cols // head_dim),
        grid=(n // head_dim,),
        in_specs=[pl.BlockSpec((None, k, head_dim), lambda j: (layer, 0, j)),
                  pl.BlockSpec((head_dim, head_dim), lambda j: (0, 0))],
        out_specs=pl.BlockSpec((k, head_dim), lambda j: (0, j)),
        out_shape=jax.ShapeDtypeStruct((k, n), w.dtype),
        compiler_params=_params(("arbitrary",)),
        name=name,
    )(w, perm)


def kernel(x, mem, ssd_in_w, ssd_conv_w, ssd_conv_b, ssd_dt_bias, ssd_a_log, ssd_d, ssd_norm_w,
           ssd_out_w, ret_in_w, ret_out_w, mixer_norm_w, mem_q_norm_w, mem_kv_norm_w, mem_q_w,
           mem_kv_w, mem_o_w, ffn_norm_w, ffn_in_w, ffn_out_w, final_norm_w):
    batch, seq, d = x.shape
    depth = mixer_norm_w.shape[0]
    xf = x.reshape(batch * seq, d)
    memf = mem.reshape(-1, d)
    ssd_inner = ssd_norm_w.shape[1]
    ssd_main = 2 * ssd_inner + 2 * SSD_GROUPS * SSD_STATE
    ssd_heads = ssd_inner // SSD_HEAD_DIM
    ret_qk = ret_in_w.shape[2] // 6
    ret_v = 2 * ret_qk

    ssd_in_b, ssd_out_b = ssd_in_w.astype(BF16), ssd_out_w.astype(BF16)
    ret_in_b, ret_out_b = ret_in_w.astype(BF16), ret_out_w.astype(BF16)
    mem_q_b, mem_kv_b, mem_o_b = mem_q_w.astype(BF16), mem_kv_w.astype(BF16), mem_o_w.astype(BF16)
    ffn_in_b, ffn_out_b = ffn_in_w.astype(BF16), ffn_out_w.astype(BF16)

    for i in range(depth):
        j = i // 2
        if i % 2 == 0:
            w_dt = jnp.pad(ssd_in_b[j, :, ssd_main:], ((0, 0), (0, LANES - ssd_heads)))
            zx, dt_raw = norm_matmul_extra(xf, mixer_norm_w[i], ssd_in_b, j, w_dt, n_out=ssd_main,
                                           tm=1024, tn=1024, out_dtype=BF16, name=f"ssd_in_{i}")
            y = ssd_scan(zx, dt_raw, ssd_conv_w[j], ssd_conv_b[j], ssd_dt_bias[j], ssd_a_log[j],
                         ssd_d[j], ssd_norm_w[j], batch=batch, name=f"ssd_scan_{i}")
            xf = matmul_residual(y, ssd_out_b, j, xf, tm=1024, tn=1024, tk=y.shape[1],
                                 name=f"ssd_out_{i}")
        else:
            w_in = deinterleave_qk_columns(ret_in_b, j, qk_cols=2 * ret_qk,
                                           head_dim=ret_qk // RET_HEADS, name=f"ret_perm_{i}")
            proj = norm_matmul(xf, mixer_norm_w[i], w_in[None], 0, n_out=w_in.shape[1], tm=1024,
                               tn=1024, out_dtype=BF16, name=f"ret_in_{i}")
            y = retention_scan(proj, batch=batch, qk_dim=ret_qk, v_dim=ret_v,
                               name=f"ret_scan_{i}")
            xf = matmul_residual(y, ret_out_b, j, xf, tm=1024, tn=1024, tk=y.shape[1],
                                 name=f"ret_out_{i}")

        kv = norm_matmul(memf, mem_kv_norm_w[i], mem_kv_b, i, n_out=2 * d, tm=512, tn=1024,
                         out_dtype=BF16, name=f"mem_kv_{i}")
        xf = memory_block(xf, mem_q_norm_w[i], mem_q_b, mem_o_b, i, kv, batch=batch,
                          heads=MEM_HEADS, tm=512, name=f"mem_blk_{i}")

        hid = norm_glu(xf, ffn_norm_w[i], ffn_in_b, i, tm=1024, tn=512, name=f"ffn_in_{i}")
        xf = matmul_residual(hid, ffn_out_b, i, xf, tm=1024, tn=1024,
                             tk=hid.shape[1] // 2, name=f"ffn_out_{i}")

    out = rms_norm_rows(xf, final_norm_w, tm=512, name="final_norm")
    return out.reshape(batch, seq, d)
```

```python
import functools

import numpy as np
import jax
import jax.numpy as jnp
from jax import lax
from jax.experimental import pallas as pl
from jax.experimental.pallas import tpu as pltpu

NORM_EPS = 1e-6
CHUNK = 128
LANES = 128
SUBLANES = 8
VMEM_LIMIT_BYTES = 56 * 1024 * 1024

SSD_HEAD_DIM = 64
SSD_GROUPS = 8
SSD_STATE = 128
SSD_CONV = 4
RET_HEADS = 8
MEM_HEADS = 4
ROPE_BASE = 10000.0

F32 = jnp.float32
BF16 = jnp.bfloat16


def _params(semantics):
    return pltpu.CompilerParams(dimension_semantics=semantics,
                                vmem_limit_bytes=VMEM_LIMIT_BYTES)


def _silu(v):
    h = 0.5 * v
    return h + h * jnp.tanh(h)


def _rms_rows(x, w):
    ms = jnp.mean(x * x, axis=-1, keepdims=True)
    return x * lax.rsqrt(ms + NORM_EPS) * w


def _norm_prologue(x_ref, nw_ref, h_ref):
    @pl.when(pl.program_id(1) == 0)
    def _():
        h_ref[...] = _rms_rows(x_ref[...], nw_ref[...]).astype(h_ref.dtype)


def _norm_mm_kernel(x_ref, nw_ref, w_ref, o_ref, h_ref):
    _norm_prologue(x_ref, nw_ref, h_ref)
    o_ref[...] = jnp.dot(h_ref[...], w_ref[...].astype(BF16),
                         preferred_element_type=F32).astype(o_ref.dtype)


def norm_matmul(x, nw, w, layer, *, n_out, tm, tn, out_dtype, name):
    m, k = x.shape
    tm = min(tm, m)
    return pl.pallas_call(
        _norm_mm_kernel,
        grid=(m // tm, n_out // tn),
        in_specs=[pl.BlockSpec((tm, k), lambda i, j: (i, 0)),
                  pl.BlockSpec((1, k), lambda i, j: (0, 0)),
                  pl.BlockSpec((None, k, tn), lambda i, j: (layer, 0, j))],
        out_specs=pl.BlockSpec((tm, tn), lambda i, j: (i, j)),
        out_shape=jax.ShapeDtypeStruct((m, n_out), out_dtype),
        scratch_shapes=[pltpu.VMEM((tm, k), BF16)],
        compiler_params=_params(("arbitrary", "arbitrary")),
        name=name,
    )(x, nw.reshape(1, k), w)


def _norm_mm_extra_kernel(x_ref, nw_ref, w_ref, we_ref, o_ref, oe_ref, h_ref):
    _norm_prologue(x_ref, nw_ref, h_ref)
    o_ref[...] = jnp.dot(h_ref[...], w_ref[...],
                         preferred_element_type=F32).astype(o_ref.dtype)

    @pl.when(pl.program_id(1) == pl.num_programs(1) - 1)
    def _():
        oe_ref[...] = jnp.dot(h_ref[...], we_ref[...], preferred_element_type=F32)


def norm_matmul_extra(x, nw, w, layer, w_extra, *, n_out, tm, tn, out_dtype, name):
    m, k = x.shape
    ne = w_extra.shape[1]
    tm = min(tm, m)
    return pl.pallas_call(
        _norm_mm_extra_kernel,
        grid=(m // tm, n_out // tn),
        in_specs=[pl.BlockSpec((tm, k), lambda i, j: (i, 0)),
                  pl.BlockSpec((1, k), lambda i, j: (0, 0)),
                  pl.BlockSpec((None, k, tn), lambda i, j: (layer, 0, j)),
                  pl.BlockSpec((k, ne), lambda i, j: (0, 0))],
        out_specs=[pl.BlockSpec((tm, tn), lambda i, j: (i, j)),
                   pl.BlockSpec((tm, ne), lambda i, j: (i, 0))],
        out_shape=[jax.ShapeDtypeStruct((m, n_out), out_dtype),
                   jax.ShapeDtypeStruct((m, ne), F32)],
        scratch_shapes=[pltpu.VMEM((tm, k), BF16)],
        compiler_params=_params(("arbitrary", "arbitrary")),
        name=name,
    )(x, nw.reshape(1, k), w, w_extra)


def _norm_glu_kernel(x_ref, nw_ref, w_ref, o_ref, h_ref):
    _norm_prologue(x_ref, nw_ref, h_ref)
    h = h_ref[...]
    g = jnp.dot(h, w_ref[0], preferred_element_type=F32)
    u = jnp.dot(h, w_ref[1], preferred_element_type=F32)
    o_ref[...] = (_silu(g) * u).astype(o_ref.dtype)


def norm_glu(x, nw, w, layer, *, tm, tn, name):
    m, k = x.shape
    hidden = w.shape[3]
    tm = min(tm, m)
    return pl.pallas_call(
        _norm_glu_kernel,
        grid=(m // tm, hidden // tn),
        in_specs=[pl.BlockSpec((tm, k), lambda i, j: (i, 0)),
                  pl.BlockSpec((1, k), lambda i, j: (0, 0)),
                  pl.BlockSpec((None, 2, k, tn), lambda i, j: (layer, 0, 0, j))],
        out_specs=pl.BlockSpec((tm, tn), lambda i, j: (i, j)),
        out_shape=jax.ShapeDtypeStruct((m, hidden), BF16),
        scratch_shapes=[pltpu.VMEM((tm, k), BF16)],
        compiler_params=_params(("arbitrary", "arbitrary")),
        name=name,
    )(x, nw.reshape(1, k), w)


def _mm_res_kernel(a_ref, w_ref, r_ref, o_ref, *, nk):
    p = jnp.dot(a_ref[...], w_ref[...], preferred_element_type=F32)
    if nk == 1:
        o_ref[...] = r_ref[...] + p
    else:
        k = pl.program_id(2)

        @pl.when(k == 0)
        def _():
            o_ref[...] = r_ref[...] + p

        @pl.when(k > 0)
        def _():
            o_ref[...] += p


def matmul_residual(a, w, layer, res, *, tm, tn, tk, name):
    m, k = a.shape
    n = w.shape[2]
    tm = min(tm, m)
    nk = k // tk
    return pl.pallas_call(
        functools.partial(_mm_res_kernel, nk=nk),
        grid=(m // tm, n // tn, nk),
        in_specs=[pl.BlockSpec((tm, tk), lambda i, j, kk: (i, kk)),
                  pl.BlockSpec((None, tk, tn), lambda i, j, kk: (layer, kk, j)),
                  pl.BlockSpec((tm, tn), lambda i, j, kk: (i, j))],
        out_specs=pl.BlockSpec((tm, tn), lambda i, j, kk: (i, j)),
        out_shape=jax.ShapeDtypeStruct((m, n), F32),
        compiler_params=_params(("arbitrary", "arbitrary", "arbitrary")),
        name=name,
    )(a, w, res)


def _mm_res_norm_kernel(a_ref, w_ref, r_ref, nw_ref, o_ref, *, nk):
    p = jnp.dot(a_ref[...], w_ref[...], preferred_element_type=F32)
    k = pl.program_id(1)
    if nk == 1:
        o_ref[...] = _rms_rows(r_ref[...] + p, nw_ref[...])
    else:
        @pl.when(k == 0)
        def _():
            o_ref[...] = r_ref[...] + p

        @pl.when((k > 0) & (k < nk - 1))
        def _():
            o_ref[...] += p

        @pl.when(k == nk - 1)
        def _():
            o_ref[...] = _rms_rows(o_ref[...] + p, nw_ref[...])


def matmul_residual_norm(a, w, layer, res, nw, *, tm, tk, name):
    m, k = a.shape
    n = w.shape[2]
    tm = min(tm, m)
    nk = k // tk
    return pl.pallas_call(
        functools.partial(_mm_res_norm_kernel, nk=nk),
        grid=(m // tm, nk),
        in_specs=[pl.BlockSpec((tm, tk), lambda i, kk: (i, kk)),
                  pl.BlockSpec((None, tk, n), lambda i, kk: (layer, kk, 0)),
                  pl.BlockSpec((tm, n), lambda i, kk: (i, 0)),
                  pl.BlockSpec((1, n), lambda i, kk: (0, 0))],
        out_specs=pl.BlockSpec((tm, n), lambda i, kk: (i, 0)),
        out_shape=jax.ShapeDtypeStruct((m, n), F32),
        compiler_params=_params(("arbitrary", "arbitrary")),
        name=name,
    )(a, w, res, nw.reshape(1, n))


def _mem_block_kernel(x_ref, nw_ref, wq_ref, k_ref, v_ref, wo_ref, o_ref, *, heads, hd):
    scale = hd ** -0.5
    x = x_ref[...]
    h = _rms_rows(x, nw_ref[...]).astype(BF16)
    q = jnp.dot(h, wq_ref[...], preferred_element_type=F32).astype(BF16)
    outs = []
    for hh in range(heads):
        sl = slice(hh * hd, (hh + 1) * hd)
        s = lax.dot_general(q[:, sl], k_ref[:, sl], (((1,), (1,)), ((), ())),
                            preferred_element_type=F32) * scale
        e = jnp.exp(s - jnp.max(s, axis=-1, keepdims=True))
        p = e / jnp.sum(e, axis=-1, keepdims=True)
        outs.append(jnp.dot(p.astype(BF16), v_ref[:, sl],
                            preferred_element_type=F32).astype(BF16))
    o = jnp.concatenate(outs, axis=1)
    o_ref[...] = x + jnp.dot(o, wo_ref[...], preferred_element_type=F32)


def memory_block(x, nw, wq, wo, layer, kv, *, batch, heads, tm, name):
    t, d = x.shape
    seq = t // batch
    mtok = kv.shape[0] // batch
    tm = min(tm, seq)
    per_batch = seq // tm
    resident = lambda: pl.BlockSpec((None, d, d), lambda i: (layer, 0, 0),
                                    pipeline_mode=pl.Buffered(1))
    return pl.pallas_call(
        functools.partial(_mem_block_kernel, heads=heads, hd=d // heads),
        grid=(t // tm,),
        in_specs=[pl.BlockSpec((tm, d), lambda i: (i, 0)),
                  pl.BlockSpec((1, d), lambda i: (0, 0)),
                  resident(),
                  pl.BlockSpec((mtok, d), lambda i: (i // per_batch, 0)),
                  pl.BlockSpec((mtok, d), lambda i: (i // per_batch, 1)),
                  resident()],
        out_specs=pl.BlockSpec((tm, d), lambda i: (i, 0)),
        out_shape=jax.ShapeDtypeStruct((t, d), F32),
        compiler_params=_params(("arbitrary",)),
        name=name,
    )(x, nw.reshape(1, d), wq, kv, kv, wo)


_SPLIT_TERMS = 3


def _split_bf16_terms(v):
    terms = []
    r = v
    for _ in range(_SPLIT_TERMS):
        t = r.astype(BF16).astype(F32)
        terms.append(t)
        r = r - t
    return terms


def _ssd_selector(hpg, hdim):
    sel = np.zeros((LANES, hpg * LANES + hpg * hdim), np.float32)
    for t in range(_SPLIT_TERMS):
        for e in range(hpg):
            sel[t * hpg + e, e * LANES:(e + 1) * LANES] = 1.0
            base = hpg * LANES
            sel[(_SPLIT_TERMS + t) * hpg + e, base + e * hdim:base + (e + 1) * hdim] = 1.0
    return jnp.asarray(sel, dtype=BF16)


def _conv_shift_matrix():
    m = np.zeros((SSD_CONV * CHUNK, 2 * CHUNK), np.float32)
    for k in range(SSD_CONV):
        for t in range(CHUNK):
            m[k * CHUNK + t, CHUNK + t - (SSD_CONV - 1) + k] = 1.0
    return jnp.asarray(m, dtype=BF16)


def _ssd_sequence(z_ref, x_ref, b_ref, c_ref, dt_ref,
                  cwx_ref, cbx_ref, cwb_ref, cbb_ref, cwc_ref, cbc_ref,
                  dtb_ref, alog_ref, dsk_ref, nw_ref, sel_ref, shf_ref, o_ref,
                  prev_x, prev_b, prev_c, xs_s, bs_s, cs_s, bcast_s, cb_s, ea_s, act_s, dtt_s, st_s, y_s,
                  *, groups, hpg, hdim, nstate):
    L = CHUNK
    gw = hpg * hdim
    assert 2 * hdim == LANES and nstate == LANES and hpg == SUBLANES

    def conv_silu(src_ref, w_ref, bias_ref, prev_ref, lo, width):
        cols = slice(lo, lo + width)
        u = src_ref[:, cols]
        taps = jnp.dot(shf_ref[...], jnp.concatenate([prev_ref[:, cols], u], axis=0),
                       preferred_element_type=F32)
        prev_ref[:, cols] = u
        acc = bias_ref[:, cols] + w_ref[0:1, cols] * taps[0:L]
        for k in range(1, SSD_CONV):
            acc = acc + w_ref[k:k + 1, cols] * taps[k * L:(k + 1) * L]
        return _silu(acc)

    for g in range(groups):
        for j in range(gw // (2 * LANES)):
            cols = slice(j * 2 * LANES, (j + 1) * 2 * LANES)
            xs_s[g, :, cols] = conv_silu(x_ref, cwx_ref, cbx_ref, prev_x,
                                         g * gw + j * 2 * LANES, 2 * LANES)
    for g in range(groups // 2):
        for ref, w_ref, bias_ref, prev_ref, dst in ((b_ref, cwb_ref, cbb_ref, prev_b, bs_s),
                                                    (c_ref, cwc_ref, cbc_ref, prev_c, cs_s)):
            pair = conv_silu(ref, w_ref, bias_ref, prev_ref, 2 * g * nstate, 2 * nstate)
            dst[2 * g] = pair[:, :nstate].astype(BF16)
            dst[2 * g + 1] = pair[:, nstate:].astype(BF16)

    dt_in = dt_ref[...] + dtb_ref[...]
    dt = jnp.maximum(dt_in, 0.0) + jnp.log1p(jnp.exp(-jnp.abs(dt_in)))
    dta = dt * (-jnp.exp(alog_ref[...]))
    row = lax.broadcasted_iota(jnp.int32, (L, LANES), 0)
    lane = lax.broadcasted_iota(jnp.int32, (L, LANES), 1)
    acum = dta
    d = 1
    while d < L:
        acum = acum + jnp.where(row >= d, pltpu.roll(acum, d, 0), 0.0)
        d *= 2
    act_s[...] = acum.T
    dtt_s[...] = dt.T
    dtw = dt * jnp.exp(acum[L - 1:L, :] - acum)
    ea = jnp.exp(acum)
    pieces = _split_bf16_terms(acum) + _split_bf16_terms(dtw)
    for g in range(groups):
        packed = jnp.zeros((L, LANES), F32)
        for t, piece in enumerate(pieces):
            shift = (hpg * t - hpg * g) % LANES
            moved = pltpu.roll(piece, shift, 1) if shift else piece
            packed = jnp.where((lane >= hpg * t) & (lane < hpg * (t + 1)), moved, packed)
        bcast_s[g] = jnp.dot(packed.astype(BF16), sel_ref[...], preferred_element_type=F32)
        cb_s[g] = lax.dot_general(cs_s[g], bs_s[g], (((1,), (1,)), ((), ())),
                                  preferred_element_type=F32)
        shift = (LANES - hpg * g) % LANES
        ea_s[g] = pltpu.roll(ea, shift, 1) if shift else ea

    first_head = lane < hdim
    causal = (lax.broadcasted_iota(jnp.int32, (L, L), 0)
              >= lax.broadcasted_iota(jnp.int32, (L, L), 1))
    n_acol = hpg * LANES

    def group_step(g):
        xs = xs_s[g]
        bc = bs_s[g]
        cc = cs_s[g]
        eag = ea_s[g]
        rows = pl.ds(pl.multiple_of(g * hpg, hpg), hpg)
        at = act_s[rows, :]
        dtt = dtt_s[rows, :]
        st = st_s[g]
        cb = cb_s[g]
        ccf = cc.astype(F32)
        ys, cds = [], []
        for j in range(hpg // 2):
            tile = slice(j * LANES, (j + 1) * LANES)
            lhs_parts = []
            for e in (2 * j, 2 * j + 1):
                acol = bcast_s[g, :, e * LANES:(e + 1) * LANES]
                decay = jnp.exp(jnp.where(causal, acol - at[e:e + 1, :], -jnp.inf))
                lhs_parts.append(jnp.concatenate(
                    [(cb * decay * dtt[e:e + 1, :]).astype(BF16),
                     (ccf * eag[:, e:e + 1]).astype(BF16)], axis=1))
            lhs = jnp.concatenate(lhs_parts, axis=0)
            rhs = jnp.concatenate([xs[:, tile].astype(BF16), st[:, tile].astype(BF16)], axis=0)
            yy = jnp.dot(lhs, rhs, preferred_element_type=F32)
            ys.append(jnp.where(first_head, yy[:L], yy[L:]))
            cds.append(jnp.where(first_head[0:1], eag[L - 1:L, 2 * j:2 * j + 1],
                                 eag[L - 1:L, 2 * j + 1:2 * j + 2]))
        y = jnp.concatenate(ys, axis=1)
        cd = jnp.concatenate(cds, axis=1)
        xdtw = (xs * bcast_s[g, :, n_acol:]).astype(BF16)
        contrib = lax.dot_general(bc, xdtw, (((0,), (0,)), ((), ())),
                                  preferred_element_type=F32)
        st_s[g] = st * cd + contrib
        y_s[g] = y + xs * dsk_ref[g]

    def epilogue():
        for g in range(groups):
            cols = slice(g * gw, (g + 1) * gw)
            y = y_s[g] * _silu(z_ref[:, cols].astype(F32))
            y = y * lax.rsqrt(jnp.mean(y * y, axis=-1, keepdims=True) + NORM_EPS)
            o_ref[:, cols] = (y * nw_ref[:, cols]).astype(o_ref.dtype)

    return group_step, epilogue


def _ssd_kernel(z_all, x_all, b_all, c_all, dt_all, *refs, batch, groups, **dims):
    n_shared = 12
    shared, o_all, scratch = refs[:n_shared], refs[n_shared], refs[n_shared + 1:]
    prev_x, prev_b, prev_c, st_all = scratch[0], scratch[1], scratch[2], scratch[11]

    @pl.when(pl.program_id(0) == 0)
    def _():
        for ref in (prev_x, prev_b, prev_c, st_all):
            ref[...] = jnp.zeros_like(ref)

    sequences = [
        _ssd_sequence(z_all.at[b], x_all.at[b], b_all.at[b], c_all.at[b], dt_all.at[b], *shared,
                      o_all.at[b], *[s.at[b] for s in scratch], groups=groups, **dims)
        for b in range(batch)]

    def group_body(g, carry):
        for group_step, _ in sequences:
            group_step(g)
        return carry

    lax.fori_loop(0, groups, group_body, 0, unroll=2)
    for _, epilogue in sequences:
        epilogue()


def ssd_scan(zx, dt_raw, conv_w, conv_b, dt_bias, a_log, d_skip, norm_w, *, batch, name):
    t = zx.shape[0]
    seq = t // batch
    nc = seq // CHUNK
    groups, nstate, hdim = SSD_GROUPS, SSD_STATE, SSD_HEAD_DIM
    d_inner = norm_w.shape[0]
    heads = d_inner // hdim
    hpg = heads // groups
    gw = hpg * hdim
    gn = groups * nstate
    assert d_inner % gn == 0 and zx.shape[1] == 2 * d_inner + 2 * gn
    assert 2 * _SPLIT_TERMS * hpg <= LANES
    nb = d_inner // gn

    cwx, cwb, cwc = conv_w[:, :d_inner], conv_w[:, d_inner:d_inner + gn], conv_w[:, d_inner + gn:]
    cbx, cbb, cbc = (conv_b[None, :d_inner], conv_b[None, d_inner:d_inner + gn],
                     conv_b[None, d_inner + gn:])
    pad = LANES - heads
    dtb = jnp.pad(dt_bias, (0, pad)).reshape(1, LANES)
    alog = jnp.pad(a_log, (0, pad)).reshape(1, LANES)
    dsk = jnp.repeat(d_skip, hdim).reshape(groups, 1, gw)
    nw = norm_w.reshape(1, d_inner)
    sel = _ssd_selector(hpg, hdim)
    shf = _conv_shift_matrix()

    full = lambda shape: pl.BlockSpec(shape, lambda c: (0,) * len(shape))
    per_seq = lambda *shape: pltpu.VMEM((batch,) + shape[:-1], shape[-1])
    zx3 = zx.reshape(batch, seq, zx.shape[1])
    out = pl.pallas_call(
        functools.partial(_ssd_kernel, batch=batch, groups=groups, hpg=hpg, hdim=hdim,
                          nstate=nstate),
        grid=(nc,),
        in_specs=[pl.BlockSpec((batch, CHUNK, d_inner), lambda c: (0, c, 0)),
                  pl.BlockSpec((batch, CHUNK, d_inner), lambda c: (0, c, 1)),
                  pl.BlockSpec((batch, CHUNK, gn), lambda c: (0, c, 2 * nb)),
                  pl.BlockSpec((batch, CHUNK, gn), lambda c: (0, c, 2 * nb + 1)),
                  pl.BlockSpec((batch, CHUNK, LANES), lambda c: (0, c, 0)),
                  full((SSD_CONV, d_inner)), full((1, d_inner)),
                  full((SSD_CONV, gn)), full((1, gn)),
                  full((SSD_CONV, gn)), full((1, gn)),
                  full((1, LANES)), full((1, LANES)),
                  full((groups, 1, gw)), full((1, d_inner)), full(sel.shape),
                  full(shf.shape)],
        out_specs=pl.BlockSpec((batch, CHUNK, d_inner), lambda c: (0, c, 0)),
        out_shape=jax.ShapeDtypeStruct((batch, seq, d_inner), BF16),
        scratch_shapes=[per_seq(CHUNK, d_inner, BF16),
                        per_seq(CHUNK, gn, BF16),
                        per_seq(CHUNK, gn, BF16),
                        per_seq(groups, CHUNK, gw, F32),
                        per_seq(groups, CHUNK, nstate, BF16),
                        per_seq(groups, CHUNK, nstate, BF16),
                        per_seq(groups, CHUNK, sel.shape[1], F32),
                        per_seq(groups, CHUNK, CHUNK, F32),
                        per_seq(groups, CHUNK, LANES, F32),
                        per_seq(LANES, CHUNK, F32),
                        per_seq(LANES, CHUNK, F32),
                        per_seq(groups, nstate, gw, F32),
                        per_seq(groups, CHUNK, gw, F32)],
        compiler_params=_params(("arbitrary",)),
        name=name,
    )(zx3, zx3, zx3, zx3, dt_raw.reshape(batch, seq, LANES), cwx, cbx, cwb, cbb, cwc, cbc,
      dtb, alog, dsk, nw, sel, shf)
    return out.reshape(t, d_inner)


def _ret_kernel(q_ref, k_ref, v_ref, g_ref, cos_ref, sin_ref,
                idec_ref, qdec_ref, kdec_ref, cdec_ref, o_ref, st_s,
                *, batch, heads, dk, dv):
    @pl.when(pl.program_id(0) == 0)
    def _():
        st_s[...] = jnp.zeros_like(st_s)

    cos = cos_ref[...]
    sin = sin_ref[...]
    half = dk // 2

    def rope(ref, b, h):
        lo = ref[b, :, h * dk:h * dk + half].astype(F32)
        hi = ref[b, :, h * dk + half:(h + 1) * dk].astype(F32)
        return jnp.concatenate([lo * cos - hi * sin, hi * cos + lo * sin], axis=1)

    for h in range(heads):
        for b in range(batch):
            qr = rope(q_ref, b, h)
            kr = rope(k_ref, b, h) * dk ** -0.5
            v = v_ref[b, :, h * dv:(h + 1) * dv]
            s = lax.dot_general(qr.astype(BF16), kr.astype(BF16), (((1,), (1,)), ((), ())),
                                preferred_element_type=F32) * idec_ref[h]
            intra = jnp.dot(s.astype(BF16), v, preferred_element_type=F32)
            st = st_s[b, h]
            cross = jnp.dot((qr * qdec_ref[h]).astype(BF16), st.astype(BF16),
                            preferred_element_type=F32)
            st_s[b, h] = st * cdec_ref[h] + lax.dot_general(
                (kr * kdec_ref[h]).astype(BF16), v, (((0,), (0,)), ((), ())),
                preferred_element_type=F32)
            y = intra + cross
            y = y * lax.rsqrt(jnp.mean(y * y, axis=-1, keepdims=True) + NORM_EPS)
            gate = g_ref[b, :, h * dv:(h + 1) * dv].astype(F32)
            o_ref[b, :, h * dv:(h + 1) * dv] = (y * _silu(gate)).astype(o_ref.dtype)


def retention_scan(proj, *, batch, qk_dim, v_dim, name):
    t = proj.shape[0]
    seq = t // batch
    nc = seq // CHUNK
    heads = RET_HEADS
    dk, dv = qk_dim // heads, v_dim // heads
    assert v_dim == 2 * qk_dim and proj.shape[1] == 2 * qk_dim + 2 * v_dim

    pos = jnp.arange(seq, dtype=F32)
    inv_freq = 1.0 / (ROPE_BASE ** jnp.linspace(0.0, 1.0, dk // 2, dtype=F32))
    theta = pos[:, None] * inv_freq[None, :]
    cos, sin = jnp.cos(theta), jnp.sin(theta)
    log_gamma = jnp.log1p(-jnp.exp2(-5.0 - jnp.arange(heads, dtype=F32)))
    idx = jnp.arange(CHUNK, dtype=F32)
    causal = jnp.tril(jnp.ones((CHUNK, CHUNK), dtype=bool))
    idec = jnp.exp(jnp.where(causal[None],
                             (idx[:, None] - idx[None, :])[None] * log_gamma[:, None, None],
                             -jnp.inf))
    qdec = jnp.exp(log_gamma[:, None] * (idx + 1.0)[None, :])[..., None]
    kdec = jnp.exp(log_gamma[:, None] * (CHUNK - 1.0 - idx)[None, :])[..., None]
    cdec = jnp.exp(CHUNK * log_gamma)[:, None, None]

    full = lambda shape: pl.BlockSpec(shape, lambda c: (0,) * len(shape))
    proj3 = proj.reshape(batch, seq, proj.shape[1])
    out = pl.pallas_call(
        functools.partial(_ret_kernel, batch=batch, heads=heads, dk=dk, dv=dv),
        grid=(nc,),
        in_specs=[pl.BlockSpec((batch, CHUNK, qk_dim), lambda c: (0, c, 0)),
                  pl.BlockSpec((batch, CHUNK, qk_dim), lambda c: (0, c, 1)),
                  pl.BlockSpec((batch, CHUNK, v_dim), lambda c: (0, c, 1)),
                  pl.BlockSpec((batch, CHUNK, v_dim), lambda c: (0, c, 2)),
                  pl.BlockSpec((CHUNK, dk // 2), lambda c: (c, 0)),
                  pl.BlockSpec((CHUNK, dk // 2), lambda c: (c, 0)),
                  full((heads, CHUNK, CHUNK)), full((heads, CHUNK, 1)),
                  full((heads, CHUNK, 1)), full((heads, 1, 1))],
        out_specs=pl.BlockSpec((batch, CHUNK, v_dim), lambda c: (0, c, 0)),
        out_shape=jax.ShapeDtypeStruct((batch, seq, v_dim), BF16),
        scratch_shapes=[pltpu.VMEM((batch, heads, dk, dv), F32)],
        compiler_params=_params(("arbitrary",)),
        name=name,
    )(proj3, proj3, proj3, proj3, cos, sin, idec, qdec, kdec, cdec)
    return out.reshape(t, v_dim)


def _deinterleave_kernel(w_ref, perm_ref, o_ref, *, n_perm):
    @pl.when(pl.program_id(0) < n_perm)
    def _():
        o_ref[...] = jnp.dot(w_ref[...].astype(BF16), perm_ref[...],
                             preferred_element_type=F32).astype(o_ref.dtype)

    @pl.when(pl.program_id(0) >= n_perm)
    def _():
        o_ref[...] = w_ref[...].astype(o_ref.dtype)


def deinterleave_qk_columns(w, layer, *, qk_cols, head_dim, name):
    k, n = w.shape[1], w.shape[2]
    src = np.arange(head_dim).reshape(head_dim // 2, 2).T.reshape(-1)
    perm = jnp.asarray(np.eye(head_dim, dtype=np.float32)[:, src], dtype=BF16)
    return pl.pallas_call(
        functools.partial(_deinterleave_kernel, n_perm=qk_cols // head_dim),
        grid=(n // head_dim,),
        in_specs=[pl.BlockSpec((None, k, head_dim), lambda j: (layer, 0, j)),
                  pl.BlockSpec((head_dim, head_dim), lambda j: (0, 0))],
        out_specs=pl.BlockSpec((k, head_dim), lambda j: (0, j)),
        out_shape=jax.ShapeDtypeStruct((k, n), BF16),
        compiler_params=_params(("arbitrary",)),
        name=name,
    )(w, perm)


def kernel(x, mem, ssd_in_w, ssd_conv_w, ssd_conv_b, ssd_dt_bias, ssd_a_log, ssd_d, ssd_norm_w,
           ssd_out_w, ret_in_w, ret_out_w, mixer_norm_w, mem_q_norm_w, mem_kv_norm_w, mem_q_w,
           mem_kv_w, mem_o_w, ffn_norm_w, ffn_in_w, ffn_out_w, final_norm_w):
    batch, seq, d = x.shape
    depth = mixer_norm_w.shape[0]
    xf = x.reshape(batch * seq, d)
    memf = mem.reshape(-1, d)
    ssd_inner = ssd_norm_w.shape[1]
    ssd_main = 2 * ssd_inner + 2 * SSD_GROUPS * SSD_STATE
    ssd_heads = ssd_inner // SSD_HEAD_DIM
    ret_qk = ret_in_w.shape[2] // 6
    ret_v = 2 * ret_qk

    ssd_in_b, ssd_out_b = ssd_in_w.astype(BF16), ssd_out_w.astype(BF16)
    ret_out_b = ret_out_w.astype(BF16)
    mem_q_b, mem_o_b = mem_q_w.astype(BF16), mem_o_w.astype(BF16)
    ffn_hidden = ffn_out_w.shape[1]
    ffn_in_b = ffn_in_w.reshape(depth, d, 2, ffn_hidden).transpose(0, 2, 1, 3).astype(BF16)
    ffn_out_b = ffn_out_w.astype(BF16)

    for i in range(depth):
        j = i // 2
        if i % 2 == 0:
            w_dt = jnp.pad(ssd_in_b[j, :, ssd_main:], ((0, 0), (0, LANES - ssd_heads)))
            zx, dt_raw = norm_matmul_extra(xf, mixer_norm_w[i], ssd_in_b, j, w_dt, n_out=ssd_main,
                                           tm=1024, tn=2048, out_dtype=BF16, name=f"ssd_in_{i}")
            y = ssd_scan(zx, dt_raw, ssd_conv_w[j], ssd_conv_b[j], ssd_dt_bias[j], ssd_a_log[j],
                         ssd_d[j], ssd_norm_w[j], batch=batch, name=f"ssd_scan_{i}")
            xf = matmul_residual(y, ssd_out_b, j, xf, tm=1024, tn=1024, tk=y.shape[1],
                                 name=f"ssd_out_{i}")
        else:
            w_in = deinterleave_qk_columns(ret_in_w, j, qk_cols=2 * ret_qk,
                                           head_dim=ret_qk // RET_HEADS, name=f"ret_perm_{i}")
            proj = norm_matmul(xf, mixer_norm_w[i], w_in[None], 0, n_out=w_in.shape[1], tm=1024,
                               tn=2048, out_dtype=BF16, name=f"ret_in_{i}")
            y = retention_scan(proj, batch=batch, qk_dim=ret_qk, v_dim=ret_v,
                               name=f"ret_scan_{i}")
            xf = matmul_residual(y, ret_out_b, j, xf, tm=1024, tn=1024, tk=y.shape[1],
                                 name=f"ret_out_{i}")

        kv = norm_matmul(memf, mem_kv_norm_w[i], mem_kv_w, i, n_out=2 * d, tm=512, tn=1024,
                         out_dtype=BF16, name=f"mem_kv_{i}")
        xf = memory_block(xf, mem_q_norm_w[i], mem_q_b, mem_o_b, i, kv, batch=batch,
                          heads=MEM_HEADS, tm=512, name=f"mem_blk_{i}")

        hid = norm_glu(xf, ffn_norm_w[i], ffn_in_b, i, tm=1024, tn=512, name=f"ffn_in_{i}")
        if i < depth - 1:
            xf = matmul_residual(hid, ffn_out_b, i, xf, tm=1024, tn=1024,
                                 tk=hid.shape[1] // 2, name=f"ffn_out_{i}")
        else:
            xf = matmul_residual_norm(hid, ffn_out_b, i, xf, final_norm_w, tm=512,
                                      tk=hid.shape[1] // 2, name=f"ffn_out_{i}")

    return xf.reshape(batch, seq, d)
```

```python
import functools

import numpy as np
import jax
import jax.numpy as jnp
from jax import lax
from jax.experimental import pallas as pl
from jax.experimental.pallas import tpu as pltpu

NORM_EPS = 1e-6
CHUNK = 128
LANES = 128
SUBLANES = 8
VMEM_LIMIT_BYTES = 56 * 1024 * 1024

SSD_HEAD_DIM = 64
SSD_GROUPS = 8
SSD_STATE = 128
SSD_CONV = 4
RET_HEADS = 8
MEM_HEADS = 4
ROPE_BASE = 10000.0

F32 = jnp.float32
BF16 = jnp.bfloat16


def _params(semantics):
    return pltpu.CompilerParams(dimension_semantics=semantics,
                                vmem_limit_bytes=VMEM_LIMIT_BYTES)


def _silu(v):
    h = 0.5 * v
    return h + h * jnp.tanh(h)


def _rms_rows(x, w):
    ms = jnp.mean(x * x, axis=-1, keepdims=True)
    return x * lax.rsqrt(ms + NORM_EPS) * w


def _norm_prologue(x_ref, nw_ref, h_ref):
    @pl.when(pl.program_id(1) == 0)
    def _():
        h_ref[...] = _rms_rows(x_ref[...], nw_ref[...]).astype(h_ref.dtype)


def _norm_mm_kernel(x_ref, nw_ref, w_ref, o_ref, h_ref):
    _norm_prologue(x_ref, nw_ref, h_ref)
    o_ref[...] = jnp.dot(h_ref[...], w_ref[...].astype(BF16),
                         preferred_element_type=F32).astype(o_ref.dtype)


def norm_matmul(x, nw, w, layer, *, n_out, tm, tn, out_dtype, name):
    m, k = x.shape
    tm = min(tm, m)
    return pl.pallas_call(
        _norm_mm_kernel,
        grid=(m // tm, n_out // tn),
        in_specs=[pl.BlockSpec((tm, k), lambda i, j: (i, 0)),
                  pl.BlockSpec((1, k), lambda i, j: (0, 0)),
                  pl.BlockSpec((None, k, tn), lambda i, j: (layer, 0, j))],
        out_specs=pl.BlockSpec((tm, tn), lambda i, j: (i, j)),
        out_shape=jax.ShapeDtypeStruct((m, n_out), out_dtype),
        scratch_shapes=[pltpu.VMEM((tm, k), BF16)],
        compiler_params=_params(("arbitrary", "arbitrary")),
        name=name,
    )(x, nw.reshape(1, k), w)


def _norm_mm_extra_kernel(x_ref, nw_ref, w_ref, we_ref, o_ref, oe_ref, h_ref):
    _norm_prologue(x_ref, nw_ref, h_ref)
    o_ref[...] = jnp.dot(h_ref[...], w_ref[...],
                         preferred_element_type=F32).astype(o_ref.dtype)

    @pl.when(pl.program_id(1) == pl.num_programs(1) - 1)
    def _():
        oe_ref[...] = jnp.dot(h_ref[...], we_ref[...], preferred_element_type=F32)


def norm_matmul_extra(x, nw, w, layer, w_extra, *, n_out, tm, tn, out_dtype, name):
    m, k = x.shape
    ne = w_extra.shape[1]
    tm = min(tm, m)
    return pl.pallas_call(
        _norm_mm_extra_kernel,
        grid=(m // tm, n_out // tn),
        in_specs=[pl.BlockSpec((tm, k), lambda i, j: (i, 0)),
                  pl.BlockSpec((1, k), lambda i, j: (0, 0)),
                  pl.BlockSpec((None, k, tn), lambda i, j: (layer, 0, j)),
                  pl.BlockSpec((k, ne), lambda i, j: (0, 0))],
        out_specs=[pl.BlockSpec((tm, tn), lambda i, j: (i, j)),
                   pl.BlockSpec((tm, ne), lambda i, j: (i, 0))],
        out_shape=[jax.ShapeDtypeStruct((m, n_out), out_dtype),
                   jax.ShapeDtypeStruct((m, ne), F32)],
        scratch_shapes=[pltpu.VMEM((tm, k), BF16)],
        compiler_params=_params(("arbitrary", "arbitrary")),
        name=name,
    )(x, nw.reshape(1, k), w, w_extra)


def _norm_glu_kernel(x_ref, nw_ref, wg_ref, wu_ref, o_ref, h_ref):
    _norm_prologue(x_ref, nw_ref, h_ref)
    h = h_ref[...]
    g = jnp.dot(h, wg_ref[...], preferred_element_type=F32)
    u = jnp.dot(h, wu_ref[...], preferred_element_type=F32)
    o_ref[...] = (_silu(g) * u).astype(o_ref.dtype)


def norm_glu(x, nw, w_gate, w_up, layer, *, tm, tn, name):
    m, k = x.shape
    hidden = w_gate.shape[2]
    tm = min(tm, m)
    w_spec = pl.BlockSpec((None, k, tn), lambda i, j: (layer, 0, j))
    return pl.pallas_call(
        _norm_glu_kernel,
        grid=(m // tm, hidden // tn),
        in_specs=[pl.BlockSpec((tm, k), lambda i, j: (i, 0)),
                  pl.BlockSpec((1, k), lambda i, j: (0, 0)),
                  w_spec, w_spec],
        out_specs=pl.BlockSpec((tm, tn), lambda i, j: (i, j)),
        out_shape=jax.ShapeDtypeStruct((m, hidden), BF16),
        scratch_shapes=[pltpu.VMEM((tm, k), BF16)],
        compiler_params=_params(("arbitrary", "arbitrary")),
        name=name,
    )(x, nw.reshape(1, k), w_gate, w_up)


def _mm_res_kernel(a_ref, w_ref, r_ref, o_ref, *, nk):
    p = jnp.dot(a_ref[...], w_ref[...], preferred_element_type=F32)
    if nk == 1:
        o_ref[...] = r_ref[...] + p
    else:
        k = pl.program_id(2)

        @pl.when(k == 0)
        def _():
            o_ref[...] = r_ref[...] + p

        @pl.when(k > 0)
        def _():
            o_ref[...] += p


def matmul_residual(a, w, layer, res, *, tm, tn, tk, name):
    m, k = a.shape
    n = w.shape[2]
    tm = min(tm, m)
    nk = k // tk
    return pl.pallas_call(
        functools.partial(_mm_res_kernel, nk=nk),
        grid=(m // tm, n // tn, nk),
        in_specs=[pl.BlockSpec((tm, tk), lambda i, j, kk: (i, kk)),
                  pl.BlockSpec((None, tk, tn), lambda i, j, kk: (layer, kk, j)),
                  pl.BlockSpec((tm, tn), lambda i, j, kk: (i, j))],
        out_specs=pl.BlockSpec((tm, tn), lambda i, j, kk: (i, j)),
        out_shape=jax.ShapeDtypeStruct((m, n), F32),
        compiler_params=_params(("arbitrary", "arbitrary", "arbitrary")),
        name=name,
    )(a, w, res)


def _mm_res_norm_kernel(a_ref, w_ref, r_ref, nw_ref, o_ref, *, nk):
    p = jnp.dot(a_ref[...], w_ref[...], preferred_element_type=F32)
    k = pl.program_id(1)
    if nk == 1:
        o_ref[...] = _rms_rows(r_ref[...] + p, nw_ref[...])
    else:
        @pl.when(k == 0)
        def _():
            o_ref[...] = r_ref[...] + p

        @pl.when((k > 0) & (k < nk - 1))
        def _():
            o_ref[...] += p

        @pl.when(k == nk - 1)
        def _():
            o_ref[...] = _rms_rows(o_ref[...] + p, nw_ref[...])


def matmul_residual_norm(a, w, layer, res, nw, *, tm, tk, name):
    m, k = a.shape
    n = w.shape[2]
    tm = min(tm, m)
    nk = k // tk
    return pl.pallas_call(
        functools.partial(_mm_res_norm_kernel, nk=nk),
        grid=(m // tm, nk),
        in_specs=[pl.BlockSpec((tm, tk), lambda i, kk: (i, kk)),
                  pl.BlockSpec((None, tk, n), lambda i, kk: (layer, kk, 0)),
                  pl.BlockSpec((tm, n), lambda i, kk: (i, 0)),
                  pl.BlockSpec((1, n), lambda i, kk: (0, 0))],
        out_specs=pl.BlockSpec((tm, n), lambda i, kk: (i, 0)),
        out_shape=jax.ShapeDtypeStruct((m, n), F32),
        compiler_params=_params(("arbitrary", "arbitrary")),
        name=name,
    )(a, w, res, nw.reshape(1, n))


def _mem_block_kernel(x_ref, nw_ref, wq_ref, k_ref, v_ref, wo_ref, o_ref, *, heads, hd):
    scale = hd ** -0.5
    x = x_ref[...]
    h = _rms_rows(x, nw_ref[...]).astype(BF16)
    q = jnp.dot(h, wq_ref[...], preferred_element_type=F32).astype(BF16)
    outs = []
    for hh in range(heads):
        sl = slice(hh * hd, (hh + 1) * hd)
        s = lax.dot_general(q[:, sl], k_ref[:, sl], (((1,), (1,)), ((), ())),
                            preferred_element_type=F32) * scale
        e = jnp.exp(s - jnp.max(s, axis=-1, keepdims=True))
        p = e / jnp.sum(e, axis=-1, keepdims=True)
        outs.append(jnp.dot(p.astype(BF16), v_ref[:, sl],
                            preferred_element_type=F32).astype(BF16))
    o = jnp.concatenate(outs, axis=1)
    o_ref[...] = x + jnp.dot(o, wo_ref[...], preferred_element_type=F32)


def memory_block(x, nw, wq, wo, layer, kv, *, batch, heads, tm, name):
    t, d = x.shape
    seq = t // batch
    mtok = kv.shape[0] // batch
    tm = min(tm, seq)
    per_batch = seq // tm
    resident = lambda: pl.BlockSpec((None, d, d), lambda i: (layer, 0, 0),
                                    pipeline_mode=pl.Buffered(1))
    return pl.pallas_call(
        functools.partial(_mem_block_kernel, heads=heads, hd=d // heads),
        grid=(t // tm,),
        in_specs=[pl.BlockSpec((tm, d), lambda i: (i, 0)),
                  pl.BlockSpec((1, d), lambda i: (0, 0)),
                  resident(),
                  pl.BlockSpec((mtok, d), lambda i: (i // per_batch, 0)),
                  pl.BlockSpec((mtok, d), lambda i: (i // per_batch, 1)),
                  resident()],
        out_specs=pl.BlockSpec((tm, d), lambda i: (i, 0)),
        out_shape=jax.ShapeDtypeStruct((t, d), F32),
        compiler_params=_params(("arbitrary",)),
        name=name,
    )(x, nw.reshape(1, d), wq, kv, kv, wo)


_SPLIT_TERMS = 3


def _split_bf16_terms(v):
    terms = []
    r = v
    for _ in range(_SPLIT_TERMS):
        t = r.astype(BF16).astype(F32)
        terms.append(t)
        r = r - t
    return terms


def _ssd_selector(hpg, hdim):
    sel = np.zeros((LANES, hpg * LANES + hpg * hdim), np.float32)
    for t in range(_SPLIT_TERMS):
        for e in range(hpg):
            sel[t * hpg + e, e * LANES:(e + 1) * LANES] = 1.0
            base = hpg * LANES
            sel[(_SPLIT_TERMS + t) * hpg + e, base + e * hdim:base + (e + 1) * hdim] = 1.0
    return jnp.asarray(sel, dtype=BF16)


def _conv_shift_matrix():
    m = np.zeros((SSD_CONV * CHUNK, 2 * CHUNK), np.float32)
    for k in range(SSD_CONV):
        for t in range(CHUNK):
            m[k * CHUNK + t, CHUNK + t - (SSD_CONV - 1) + k] = 1.0
    return jnp.asarray(m, dtype=BF16)


def _ssd_sequence(z_ref, x_ref, b_ref, c_ref, dt_ref,
                  cwx_ref, cbx_ref, cwb_ref, cbb_ref, cwc_ref, cbc_ref,
                  dtb_ref, alog_ref, dsk_ref, nw_ref, sel_ref, shf_ref, o_ref,
                  prev_x, prev_b, prev_c, xs_s, bs_s, cs_s, bcast_s, cb_s, ea_s, act_s, dtt_s, st_s, y_s,
                  *, groups, hpg, hdim, nstate):
    L = CHUNK
    gw = hpg * hdim
    assert 2 * hdim == LANES and nstate == LANES and hpg == SUBLANES

    def conv_silu(src_ref, w_ref, bias_ref, prev_ref, lo, width):
        cols = slice(lo, lo + width)
        u = src_ref[:, cols]
        taps = jnp.dot(shf_ref[...], jnp.concatenate([prev_ref[:, cols], u], axis=0),
                       preferred_element_type=F32)
        prev_ref[:, cols] = u
        acc = bias_ref[:, cols] + w_ref[0:1, cols] * taps[0:L]
        for k in range(1, SSD_CONV):
            acc = acc + w_ref[k:k + 1, cols] * taps[k * L:(k + 1) * L]
        return _silu(acc)

    for g in range(groups):
        for j in range(gw // (2 * LANES)):
            cols = slice(j * 2 * LANES, (j + 1) * 2 * LANES)
            xs_s[g, :, cols] = conv_silu(x_ref, cwx_ref, cbx_ref, prev_x,
                                         g * gw + j * 2 * LANES, 2 * LANES)
    for g in range(groups // 2):
        for ref, w_ref, bias_ref, prev_ref, dst in ((b_ref, cwb_ref, cbb_ref, prev_b, bs_s),
                                                    (c_ref, cwc_ref, cbc_ref, prev_c, cs_s)):
            pair = conv_silu(ref, w_ref, bias_ref, prev_ref, 2 * g * nstate, 2 * nstate)
            dst[2 * g] = pair[:, :nstate].astype(BF16)
            dst[2 * g + 1] = pair[:, nstate:].astype(BF16)

    dt_in = dt_ref[...] + dtb_ref[...]
    dt = jnp.maximum(dt_in, 0.0) + jnp.log1p(jnp.exp(-jnp.abs(dt_in)))
    dta = dt * (-jnp.exp(alog_ref[...]))
    row = lax.broadcasted_iota(jnp.int32, (L, LANES), 0)
    lane = lax.broadcasted_iota(jnp.int32, (L, LANES), 1)
    acum = dta
    d = 1
    while d < L:
        acum = acum + jnp.where(row >= d, pltpu.roll(acum, d, 0), 0.0)
        d *= 2
    act_s[...] = acum.T
    dtt_s[...] = dt.T
    dtw = dt * jnp.exp(acum[L - 1:L, :] - acum)
    ea = jnp.exp(acum)
    pieces = _split_bf16_terms(acum) + _split_bf16_terms(dtw)
    for g in range(groups):
        packed = jnp.zeros((L, LANES), F32)
        for t, piece in enumerate(pieces):
            shift = (hpg * t - hpg * g) % LANES
            moved = pltpu.roll(piece, shift, 1) if shift else piece
            packed = jnp.where((lane >= hpg * t) & (lane < hpg * (t + 1)), moved, packed)
        bcast_s[g] = jnp.dot(packed.astype(BF16), sel_ref[...], preferred_element_type=F32)
        cb_s[g] = lax.dot_general(cs_s[g], bs_s[g], (((1,), (1,)), ((), ())),
                                  preferred_element_type=F32)
        shift = (LANES - hpg * g) % LANES
        ea_s[g] = pltpu.roll(ea, shift, 1) if shift else ea

    first_head = lane < hdim
    causal = (lax.broadcasted_iota(jnp.int32, (L, L), 0)
              >= lax.broadcasted_iota(jnp.int32, (L, L), 1))
    n_acol = hpg * LANES

    def group_step(g):
        xs = xs_s[g]
        bc = bs_s[g]
        cc = cs_s[g]
        eag = ea_s[g]
        rows = pl.ds(pl.multiple_of(g * hpg, hpg), hpg)
        at = act_s[rows, :]
        dtt = dtt_s[rows, :]
        st = st_s[g]
        cb = cb_s[g]
        ccf = cc.astype(F32)
        ys, cds = [], []
        for j in range(hpg // 2):
            tile = slice(j * LANES, (j + 1) * LANES)
            lhs_parts = []
            for e in (2 * j, 2 * j + 1):
                acol = bcast_s[g, :, e * LANES:(e + 1) * LANES]
                decay = jnp.exp(jnp.where(causal, acol - at[e:e + 1, :], -jnp.inf))
                lhs_parts.append(jnp.concatenate(
                    [(cb * decay * dtt[e:e + 1, :]).astype(BF16),
                     (ccf * eag[:, e:e + 1]).astype(BF16)], axis=1))
            lhs = jnp.concatenate(lhs_parts, axis=0)
            rhs = jnp.concatenate([xs[:, tile].astype(BF16), st[:, tile].astype(BF16)], axis=0)
            yy = jnp.dot(lhs, rhs, preferred_element_type=F32)
            ys.append(jnp.where(first_head, yy[:L], yy[L:]))
            cds.append(jnp.where(first_head[0:1], eag[L - 1:L, 2 * j:2 * j + 1],
                                 eag[L - 1:L, 2 * j + 1:2 * j + 2]))
        y = jnp.concatenate(ys, axis=1)
        cd = jnp.concatenate(cds, axis=1)
        xdtw = (xs * bcast_s[g, :, n_acol:]).astype(BF16)
        contrib = lax.dot_general(bc, xdtw, (((0,), (0,)), ((), ())),
                                  preferred_element_type=F32)
        st_s[g] = st * cd + contrib
        y_s[g] = y + xs * dsk_ref[g]

    def epilogue():
        for g in range(groups):
            cols = slice(g * gw, (g + 1) * gw)
            y = y_s[g] * _silu(z_ref[:, cols].astype(F32))
            y = y * lax.rsqrt(jnp.mean(y * y, axis=-1, keepdims=True) + NORM_EPS)
            o_ref[:, cols] = (y * nw_ref[:, cols]).astype(o_ref.dtype)

    return group_step, epilogue


def _ssd_kernel(z_all, x_all, b_all, c_all, dt_all, *refs, batch, groups, **dims):
    n_shared = 12
    shared, o_all, scratch = refs[:n_shared], refs[n_shared], refs[n_shared + 1:]
    prev_x, prev_b, prev_c, st_all = scratch[0], scratch[1], scratch[2], scratch[11]

    @pl.when(pl.program_id(0) == 0)
    def _():
        for ref in (prev_x, prev_b, prev_c, st_all):
            ref[...] = jnp.zeros_like(ref)

    sequences = [
        _ssd_sequence(z_all.at[b], x_all.at[b], b_all.at[b], c_all.at[b], dt_all.at[b], *shared,
                      o_all.at[b], *[s.at[b] for s in scratch], groups=groups, **dims)
        for b in range(batch)]

    def group_body(g, carry):
        for group_step, _ in sequences:
            group_step(g)
        return carry

    lax.fori_loop(0, groups, group_body, 0, unroll=2)
    for _, epilogue in sequences:
        epilogue()


def ssd_scan(zx, dt_raw, conv_w, conv_b, dt_bias, a_log, d_skip, norm_w, *, batch, name):
    t = zx.shape[0]
    seq = t // batch
    nc = seq // CHUNK
    groups, nstate, hdim = SSD_GROUPS, SSD_STATE, SSD_HEAD_DIM
    d_inner = norm_w.shape[0]
    heads = d_inner // hdim
    hpg = heads // groups
    gw = hpg * hdim
    gn = groups * nstate
    assert d_inner % gn == 0 and zx.shape[1] == 2 * d_inner + 2 * gn
    assert 2 * _SPLIT_TERMS * hpg <= LANES
    nb = d_inner // gn

    cwx, cwb, cwc = conv_w[:, :d_inner], conv_w[:, d_inner:d_inner + gn], conv_w[:, d_inner + gn:]
    cbx, cbb, cbc = (conv_b[None, :d_inner], conv_b[None, d_inner:d_inner + gn],
                     conv_b[None, d_inner + gn:])
    pad = LANES - heads
    dtb = jnp.pad(dt_bias, (0, pad)).reshape(1, LANES)
    alog = jnp.pad(a_log, (0, pad)).reshape(1, LANES)
    dsk = jnp.repeat(d_skip, hdim).reshape(groups, 1, gw)
    nw = norm_w.reshape(1, d_inner)
    sel = _ssd_selector(hpg, hdim)
    shf = _conv_shift_matrix()

    full = lambda shape: pl.BlockSpec(shape, lambda c: (0,) * len(shape))
    per_seq = lambda *shape: pltpu.VMEM((batch,) + shape[:-1], shape[-1])
    zx3 = zx.reshape(batch, seq, zx.shape[1])
    out = pl.pallas_call(
        functools.partial(_ssd_kernel, batch=batch, groups=groups, hpg=hpg, hdim=hdim,
                          nstate=nstate),
        grid=(nc,),
        in_specs=[pl.BlockSpec((batch, CHUNK, d_inner), lambda c: (0, c, 0)),
                  pl.BlockSpec((batch, CHUNK, d_inner), lambda c: (0, c, 1)),
                  pl.BlockSpec((batch, CHUNK, gn), lambda c: (0, c, 2 * nb)),
                  pl.BlockSpec((batch, CHUNK, gn), lambda c: (0, c, 2 * nb + 1)),
                  pl.BlockSpec((batch, CHUNK, LANES), lambda c: (0, c, 0)),
                  full((SSD_CONV, d_inner)), full((1, d_inner)),
                  full((SSD_CONV, gn)), full((1, gn)),
                  full((SSD_CONV, gn)), full((1, gn)),
                  full((1, LANES)), full((1, LANES)),
                  full((groups, 1, gw)), full((1, d_inner)), full(sel.shape),
                  full(shf.shape)],
        out_specs=pl.BlockSpec((batch, CHUNK, d_inner), lambda c: (0, c, 0)),
        out_shape=jax.ShapeDtypeStruct((batch, seq, d_inner), BF16),
        scratch_shapes=[per_seq(CHUNK, d_inner, BF16),
                        per_seq(CHUNK, gn, BF16),
                        per_seq(CHUNK, gn, BF16),
                        per_seq(groups, CHUNK, gw, F32),
                        per_seq(groups, CHUNK, nstate, BF16),
                        per_seq(groups, CHUNK, nstate, BF16),
                        per_seq(groups, CHUNK, sel.shape[1], F32),
                        per_seq(groups, CHUNK, CHUNK, F32),
                        per_seq(groups, CHUNK, LANES, F32),
                        per_seq(LANES, CHUNK, F32),
                        per_seq(LANES, CHUNK, F32),
                        per_seq(groups, nstate, gw, F32),
                        per_seq(groups, CHUNK, gw, F32)],
        compiler_params=_params(("arbitrary",)),
        name=name,
    )(zx3, zx3, zx3, zx3, dt_raw.reshape(batch, seq, LANES), cwx, cbx, cwb, cbb, cwc, cbc,
      dtb, alog, dsk, nw, sel, shf)
    return out.reshape(t, d_inner)


def _ret_kernel(q_ref, k_ref, v_ref, g_ref, cos_ref, sin_ref,
                idec_ref, qdec_ref, kdec_ref, cdec_ref, o_ref, st_s,
                *, batch, heads, dk, dv):
    @pl.when(pl.program_id(0) == 0)
    def _():
        st_s[...] = jnp.zeros_like(st_s)

    cos = cos_ref[...]
    sin = sin_ref[...]
    half = dk // 2

    def rope(ref, b, h):
        lo = ref[b, :, h * dk:h * dk + half].astype(F32)
        hi = ref[b, :, h * dk + half:(h + 1) * dk].astype(F32)
        return jnp.concatenate([lo * cos - hi * sin, hi * cos + lo * sin], axis=1)

    for h in range(heads):
        for b in range(batch):
            qr = rope(q_ref, b, h)
            kr = rope(k_ref, b, h) * dk ** -0.5
            v = v_ref[b, :, h * dv:(h + 1) * dv]
            s = lax.dot_general(qr.astype(BF16), kr.astype(BF16), (((1,), (1,)), ((), ())),
                                preferred_element_type=F32) * idec_ref[h]
            intra = jnp.dot(s.astype(BF16), v, preferred_element_type=F32)
            st = st_s[b, h]
            cross = jnp.dot((qr * qdec_ref[h]).astype(BF16), st.astype(BF16),
                            preferred_element_type=F32)
            st_s[b, h] = st * cdec_ref[h] + lax.dot_general(
                (kr * kdec_ref[h]).astype(BF16), v, (((0,), (0,)), ((), ())),
                preferred_element_type=F32)
            y = intra + cross
            y = y * lax.rsqrt(jnp.mean(y * y, axis=-1, keepdims=True) + NORM_EPS)
            gate = g_ref[b, :, h * dv:(h + 1) * dv].astype(F32)
            o_ref[b, :, h * dv:(h + 1) * dv] = (y * _silu(gate)).astype(o_ref.dtype)


def retention_scan(proj, *, batch, qk_dim, v_dim, name):
    t = proj.shape[0]
    seq = t // batch
    nc = seq // CHUNK
    heads = RET_HEADS
    dk, dv = qk_dim // heads, v_dim // heads
    assert v_dim == 2 * qk_dim and proj.shape[1] == 2 * qk_dim + 2 * v_dim

    pos = jnp.arange(seq, dtype=F32)
    inv_freq = 1.0 / (ROPE_BASE ** jnp.linspace(0.0, 1.0, dk // 2, dtype=F32))
    theta = pos[:, None] * inv_freq[None, :]
    cos, sin = jnp.cos(theta), jnp.sin(theta)
    log_gamma = jnp.log1p(-jnp.exp2(-5.0 - jnp.arange(heads, dtype=F32)))
    idx = jnp.arange(CHUNK, dtype=F32)
    causal = jnp.tril(jnp.ones((CHUNK, CHUNK), dtype=bool))
    idec = jnp.exp(jnp.where(causal[None],
                             (idx[:, None] - idx[None, :])[None] * log_gamma[:, None, None],
                             -jnp.inf))
    qdec = jnp.exp(log_gamma[:, None] * (idx + 1.0)[None, :])[..., None]
    kdec = jnp.exp(log_gamma[:, None] * (CHUNK - 1.0 - idx)[None, :])[..., None]
    cdec = jnp.exp(CHUNK * log_gamma)[:, None, None]

    full = lambda shape: pl.BlockSpec(shape, lambda c: (0,) * len(shape))
    proj3 = proj.reshape(batch, seq, proj.shape[1])
    out = pl.pallas_call(
        functools.partial(_ret_kernel, batch=batch, heads=heads, dk=dk, dv=dv),
        grid=(nc,),
        in_specs=[pl.BlockSpec((batch, CHUNK, qk_dim), lambda c: (0, c, 0)),
                  pl.BlockSpec((batch, CHUNK, qk_dim), lambda c: (0, c, 1)),
                  pl.BlockSpec((batch, CHUNK, v_dim), lambda c: (0, c, 1)),
                  pl.BlockSpec((batch, CHUNK, v_dim), lambda c: (0, c, 2)),
                  pl.BlockSpec((CHUNK, dk // 2), lambda c: (c, 0)),
                  pl.BlockSpec((CHUNK, dk // 2), lambda c: (c, 0)),
                  full((heads, CHUNK, CHUNK)), full((heads, CHUNK, 1)),
                  full((heads, CHUNK, 1)), full((heads, 1, 1))],
        out_specs=pl.BlockSpec((batch, CHUNK, v_dim), lambda c: (0, c, 0)),
        out_shape=jax.ShapeDtypeStruct((batch, seq, v_dim), BF16),
        scratch_shapes=[pltpu.VMEM((batch, heads, dk, dv), F32)],
        compiler_params=_params(("arbitrary",)),
        name=name,
    )(proj3, proj3, proj3, proj3, cos, sin, idec, qdec, kdec, cdec)
    return out.reshape(t, v_dim)


def _deinterleave_kernel(w_ref, perm_ref, o_ref, *, n_perm, head_dim):
    @pl.when(pl.program_id(0) < n_perm)
    def _():
        for lo in range(0, o_ref.shape[1], head_dim):
            cols = slice(lo, lo + head_dim)
            o_ref[:, cols] = jnp.dot(w_ref[:, cols].astype(BF16), perm_ref[...],
                                     preferred_element_type=F32).astype(o_ref.dtype)

    @pl.when(pl.program_id(0) >= n_perm)
    def _():
        o_ref[...] = w_ref[...].astype(o_ref.dtype)


def deinterleave_qk_columns(w, layer, *, qk_cols, head_dim, tn, name):
    k, n = w.shape[1], w.shape[2]
    assert qk_cols % tn == 0 and tn % head_dim == 0
    src = np.arange(head_dim).reshape(head_dim // 2, 2).T.reshape(-1)
    perm = jnp.asarray(np.eye(head_dim, dtype=np.float32)[:, src], dtype=BF16)
    return pl.pallas_call(
        functools.partial(_deinterleave_kernel, n_perm=qk_cols // tn, head_dim=head_dim),
        grid=(n // tn,),
        in_specs=[pl.BlockSpec((None, k, tn), lambda j: (layer, 0, j)),
                  pl.BlockSpec((head_dim, head_dim), lambda j: (0, 0))],
        out_specs=pl.BlockSpec((k, tn), lambda j: (0, j)),
        out_shape=jax.ShapeDtypeStruct((k, n), BF16),
        compiler_params=_params(("arbitrary",)),
        name=name,
    )(w, perm)


def kernel(x, mem, ssd_in_w, ssd_conv_w, ssd_conv_b, ssd_dt_bias, ssd_a_log, ssd_d, ssd_norm_w,
           ssd_out_w, ret_in_w, ret_out_w, mixer_norm_w, mem_q_norm_w, mem_kv_norm_w, mem_q_w,
           mem_kv_w, mem_o_w, ffn_norm_w, ffn_in_w, ffn_out_w, final_norm_w):
    batch, seq, d = x.shape
    depth = mixer_norm_w.shape[0]
    xf = x.reshape(batch * seq, d)
    memf = mem.reshape(-1, d)
    ssd_inner = ssd_norm_w.shape[1]
    ssd_main = 2 * ssd_inner + 2 * SSD_GROUPS * SSD_STATE
    ssd_heads = ssd_inner // SSD_HEAD_DIM
    ret_qk = ret_in_w.shape[2] // 6
    ret_v = 2 * ret_qk

    ssd_in_b, ssd_out_b = ssd_in_w[:, :, :ssd_main].astype(BF16), ssd_out_w.astype(BF16)
    ssd_dt_b = jnp.pad(ssd_in_w[:, :, ssd_main:],
                       ((0, 0), (0, 0), (0, LANES - ssd_heads))).astype(BF16)
    ret_out_b = ret_out_w.astype(BF16)
    mem_q_b, mem_o_b = mem_q_w.astype(BF16), mem_o_w.astype(BF16)
    ffn_hidden = ffn_out_w.shape[1]
    ffn_gate_b = ffn_in_w[:, :, :ffn_hidden].astype(BF16)
    ffn_up_b = ffn_in_w[:, :, ffn_hidden:].astype(BF16)
    ffn_out_b = ffn_out_w.astype(BF16)

    for i in range(depth):
        j = i // 2
        if i % 2 == 0:
            w_dt = ssd_dt_b[j]
            zx, dt_raw = norm_matmul_extra(xf, mixer_norm_w[i], ssd_in_b, j, w_dt, n_out=ssd_main,
                                           tm=1024, tn=2048, out_dtype=BF16, name=f"ssd_in_{i}")
            y = ssd_scan(zx, dt_raw, ssd_conv_w[j], ssd_conv_b[j], ssd_dt_bias[j], ssd_a_log[j],
                         ssd_d[j], ssd_norm_w[j], batch=batch, name=f"ssd_scan_{i}")
            xf = matmul_residual(y, ssd_out_b, j, xf, tm=1024, tn=1024, tk=y.shape[1],
                                 name=f"ssd_out_{i}")
        else:
            w_in = deinterleave_qk_columns(ret_in_w, j, qk_cols=2 * ret_qk,
                                           head_dim=ret_qk // RET_HEADS, tn=1024,
                                           name=f"ret_perm_{i}")
            proj = norm_matmul(xf, mixer_norm_w[i], w_in[None], 0, n_out=w_in.shape[1], tm=1024,
                               tn=2048, out_dtype=BF16, name=f"ret_in_{i}")
            y = retention_scan(proj, batch=batch, qk_dim=ret_qk, v_dim=ret_v,
                               name=f"ret_scan_{i}")
            xf = matmul_residual(y, ret_out_b, j, xf, tm=1024, tn=1024, tk=y.shape[1],
                                 name=f"ret_out_{i}")

        kv = norm_matmul(memf, mem_kv_norm_w[i], mem_kv_w, i, n_out=2 * d, tm=512, tn=1024,
                         out_dtype=BF16, name=f"mem_kv_{i}")
        xf = memory_block(xf, mem_q_norm_w[i], mem_q_b, mem_o_b, i, kv, batch=batch,
                          heads=MEM_HEADS, tm=512, name=f"mem_blk_{i}")

        hid = norm_glu(xf, ffn_norm_w[i], ffn_gate_b, ffn_up_b, i, tm=1024, tn=512,
                       name=f"ffn_in_{i}")
        if i < depth - 1:
            xf = matmul_residual(hid, ffn_out_b, i, xf, tm=1024, tn=1024,
                                 tk=hid.shape[1] // 2, name=f"ffn_out_{i}")
        else:
            xf = matmul_residual_norm(hid, ffn_out_b, i, xf, final_norm_w, tm=512,
                                      tk=hid.shape[1] // 2, name=f"ffn_out_{i}")

    return xf.reshape(batch, seq, d)
```

```python
import functools

import numpy as np
import jax
import jax.numpy as jnp
from jax import lax
from jax.experimental import pallas as pl
from jax.experimental.pallas import tpu as pltpu

NORM_EPS = 1e-6
CHUNK = 128
LANES = 128
SUBLANES = 8
VMEM_LIMIT_BYTES = 56 * 1024 * 1024

SSD_HEAD_DIM = 64
SSD_GROUPS = 8
SSD_STATE = 128
SSD_CONV = 4
RET_HEADS = 8
MEM_HEADS = 4
ROPE_BASE = 10000.0

F32 = jnp.float32
BF16 = jnp.bfloat16


def _params(semantics):
    return pltpu.CompilerParams(dimension_semantics=semantics,
                                vmem_limit_bytes=VMEM_LIMIT_BYTES)


def _silu(v):
    h = 0.5 * v
    return h + h * jnp.tanh(h)


def _rms_rows(x, w):
    ms = jnp.mean(x * x, axis=-1, keepdims=True)
    return x * lax.rsqrt(ms + NORM_EPS) * w


def _norm_prologue(x_ref, nw_ref, h_ref):
    @pl.when(pl.program_id(1) == 0)
    def _():
        h_ref[...] = _rms_rows(x_ref[...], nw_ref[...]).astype(h_ref.dtype)


def _norm_mm_kernel(x_ref, nw_ref, w_ref, o_ref, h_ref):
    _norm_prologue(x_ref, nw_ref, h_ref)
    o_ref[...] = jnp.dot(h_ref[...], w_ref[...].astype(BF16),
                         preferred_element_type=F32).astype(o_ref.dtype)


def norm_matmul(x, nw, w, layer, *, n_out, tm, tn, out_dtype, name):
    m, k = x.shape
    tm = min(tm, m)
    return pl.pallas_call(
        _norm_mm_kernel,
        grid=(m // tm, n_out // tn),
        in_specs=[pl.BlockSpec((tm, k), lambda i, j: (i, 0)),
                  pl.BlockSpec((1, k), lambda i, j: (0, 0)),
                  pl.BlockSpec((None, k, tn), lambda i, j: (layer, 0, j))],
        out_specs=pl.BlockSpec((tm, tn), lambda i, j: (i, j)),
        out_shape=jax.ShapeDtypeStruct((m, n_out), out_dtype),
        scratch_shapes=[pltpu.VMEM((tm, k), BF16)],
        compiler_params=_params(("arbitrary", "arbitrary")),
        name=name,
    )(x, nw.reshape(1, k), w)


def _norm_mm_extra_kernel(x_ref, nw_ref, w_ref, we_ref, o_ref, oe_ref, h_ref):
    _norm_prologue(x_ref, nw_ref, h_ref)
    o_ref[...] = jnp.dot(h_ref[...], w_ref[...],
                         preferred_element_type=F32).astype(o_ref.dtype)

    @pl.when(pl.program_id(1) == pl.num_programs(1) - 1)
    def _():
        oe_ref[...] = jnp.dot(h_ref[...], we_ref[...], preferred_element_type=F32)


def norm_matmul_extra(x, nw, w, layer, w_extra, *, n_out, tm, tn, out_dtype, name):
    m, k = x.shape
    ne = w_extra.shape[1]
    tm = min(tm, m)
    return pl.pallas_call(
        _norm_mm_extra_kernel,
        grid=(m // tm, n_out // tn),
        in_specs=[pl.BlockSpec((tm, k), lambda i, j: (i, 0)),
                  pl.BlockSpec((1, k), lambda i, j: (0, 0)),
                  pl.BlockSpec((None, k, tn), lambda i, j: (layer, 0, j)),
                  pl.BlockSpec((k, ne), lambda i, j: (0, 0))],
        out_specs=[pl.BlockSpec((tm, tn), lambda i, j: (i, j)),
                   pl.BlockSpec((tm, ne), lambda i, j: (i, 0))],
        out_shape=[jax.ShapeDtypeStruct((m, n_out), out_dtype),
                   jax.ShapeDtypeStruct((m, ne), F32)],
        scratch_shapes=[pltpu.VMEM((tm, k), BF16)],
        compiler_params=_params(("arbitrary", "arbitrary")),
        name=name,
    )(x, nw.reshape(1, k), w, w_extra)


def _norm_glu_kernel(x_ref, nw_ref, wg_ref, wu_ref, o_ref, h_ref):
    _norm_prologue(x_ref, nw_ref, h_ref)
    h = h_ref[...]
    g = jnp.dot(h, wg_ref[...], preferred_element_type=F32)
    u = jnp.dot(h, wu_ref[...], preferred_element_type=F32)
    o_ref[...] = (_silu(g) * u).astype(o_ref.dtype)


def norm_glu(x, nw, w_gate, w_up, layer, *, tm, tn, name):
    m, k = x.shape
    hidden = w_gate.shape[2]
    tm = min(tm, m)
    w_spec = pl.BlockSpec((None, k, tn), lambda i, j: (layer, 0, j))
    return pl.pallas_call(
        _norm_glu_kernel,
        grid=(m // tm, hidden // tn),
        in_specs=[pl.BlockSpec((tm, k), lambda i, j: (i, 0)),
                  pl.BlockSpec((1, k), lambda i, j: (0, 0)),
                  w_spec, w_spec],
        out_specs=pl.BlockSpec((tm, tn), lambda i, j: (i, j)),
        out_shape=jax.ShapeDtypeStruct((m, hidden), BF16),
        scratch_shapes=[pltpu.VMEM((tm, k), BF16)],
        compiler_params=_params(("arbitrary", "arbitrary")),
        name=name,
    )(x, nw.reshape(1, k), w_gate, w_up)


def _mm_res_kernel(a_ref, w_ref, r_ref, o_ref, *, nk):
    p = jnp.dot(a_ref[...], w_ref[...], preferred_element_type=F32)
    if nk == 1:
        o_ref[...] = r_ref[...] + p
    else:
        k = pl.program_id(2)

        @pl.when(k == 0)
        def _():
            o_ref[...] = r_ref[...] + p

        @pl.when(k > 0)
        def _():
            o_ref[...] += p


def matmul_residual(a, w, layer, res, *, tm, tn, tk, name):
    m, k = a.shape
    n = w.shape[2]
    tm = min(tm, m)
    nk = k // tk
    return pl.pallas_call(
        functools.partial(_mm_res_kernel, nk=nk),
        grid=(m // tm, n // tn, nk),
        in_specs=[pl.BlockSpec((tm, tk), lambda i, j, kk: (i, kk)),
                  pl.BlockSpec((None, tk, tn), lambda i, j, kk: (layer, kk, j)),
                  pl.BlockSpec((tm, tn), lambda i, j, kk: (i, j))],
        out_specs=pl.BlockSpec((tm, tn), lambda i, j, kk: (i, j)),
        out_shape=jax.ShapeDtypeStruct((m, n), F32),
        compiler_params=_params(("arbitrary", "arbitrary", "arbitrary")),
        name=name,
    )(a, w, res)


def _mm_res_norm_kernel(a_ref, w_ref, r_ref, nw_ref, o_ref, *, nk):
    p = jnp.dot(a_ref[...], w_ref[...], preferred_element_type=F32)
    k = pl.program_id(1)
    if nk == 1:
        o_ref[...] = _rms_rows(r_ref[...] + p, nw_ref[...])
    else:
        @pl.when(k == 0)
        def _():
            o_ref[...] = r_ref[...] + p

        @pl.when((k > 0) & (k < nk - 1))
        def _():
            o_ref[...] += p

        @pl.when(k == nk - 1)
        def _():
            o_ref[...] = _rms_rows(o_ref[...] + p, nw_ref[...])


def matmul_residual_norm(a, w, layer, res, nw, *, tm, tk, name):
    m, k = a.shape
    n = w.shape[2]
    tm = min(tm, m)
    nk = k // tk
    return pl.pallas_call(
        functools.partial(_mm_res_norm_kernel, nk=nk),
        grid=(m // tm, nk),
        in_specs=[pl.BlockSpec((tm, tk), lambda i, kk: (i, kk)),
                  pl.BlockSpec((None, tk, n), lambda i, kk: (layer, kk, 0)),
                  pl.BlockSpec((tm, n), lambda i, kk: (i, 0)),
                  pl.BlockSpec((1, n), lambda i, kk: (0, 0))],
        out_specs=pl.BlockSpec((tm, n), lambda i, kk: (i, 0)),
        out_shape=jax.ShapeDtypeStruct((m, n), F32),
        compiler_params=_params(("arbitrary", "arbitrary")),
        name=name,
    )(a, w, res, nw.reshape(1, n))


def _mem_block_kernel(x_ref, nw_ref, wq_ref, k_ref, v_ref, wo_ref, o_ref, *, heads, hd):
    scale = hd ** -0.5
    x = x_ref[...]
    h = _rms_rows(x, nw_ref[...]).astype(BF16)
    q = jnp.dot(h, wq_ref[...], preferred_element_type=F32).astype(BF16)
    outs = []
    for hh in range(heads):
        sl = slice(hh * hd, (hh + 1) * hd)
        s = lax.dot_general(q[:, sl], k_ref[:, sl], (((1,), (1,)), ((), ())),
                            preferred_element_type=F32) * scale
        e = jnp.exp(s - jnp.max(s, axis=-1, keepdims=True))
        p = e / jnp.sum(e, axis=-1, keepdims=True)
        outs.append(jnp.dot(p.astype(BF16), v_ref[:, sl],
                            preferred_element_type=F32).astype(BF16))
    o = jnp.concatenate(outs, axis=1)
    o_ref[...] = x + jnp.dot(o, wo_ref[...], preferred_element_type=F32)


def memory_block(x, nw, wq, wo, layer, kv, *, batch, heads, tm, name):
    t, d = x.shape
    seq = t // batch
    mtok = kv.shape[0] // batch
    tm = min(tm, seq)
    per_batch = seq // tm
    resident = lambda: pl.BlockSpec((None, d, d), lambda i: (layer, 0, 0),
                                    pipeline_mode=pl.Buffered(1))
    return pl.pallas_call(
        functools.partial(_mem_block_kernel, heads=heads, hd=d // heads),
        grid=(t // tm,),
        in_specs=[pl.BlockSpec((tm, d), lambda i: (i, 0)),
                  pl.BlockSpec((1, d), lambda i: (0, 0)),
                  resident(),
                  pl.BlockSpec((mtok, d), lambda i: (i // per_batch, 0)),
                  pl.BlockSpec((mtok, d), lambda i: (i // per_batch, 1)),
                  resident()],
        out_specs=pl.BlockSpec((tm, d), lambda i: (i, 0)),
        out_shape=jax.ShapeDtypeStruct((t, d), F32),
        compiler_params=_params(("arbitrary",)),
        name=name,
    )(x, nw.reshape(1, d), wq, kv, kv, wo)


_SPLIT_TERMS = 3


def _split_bf16_terms(v):
    terms = []
    r = v
    for _ in range(_SPLIT_TERMS):
        t = r.astype(BF16).astype(F32)
        terms.append(t)
        r = r - t
    return terms


def _ssd_selector(hpg, hdim):
    sel = np.zeros((LANES, hpg * LANES + hpg * hdim), np.float32)
    for t in range(_SPLIT_TERMS):
        for e in range(hpg):
            sel[t * hpg + e, e * LANES:(e + 1) * LANES] = 1.0
            base = hpg * LANES
            sel[(_SPLIT_TERMS + t) * hpg + e, base + e * hdim:base + (e + 1) * hdim] = 1.0
    return jnp.asarray(sel, dtype=BF16)


def _conv_shift_matrix():
    m = np.zeros((SSD_CONV * CHUNK, 2 * CHUNK), np.float32)
    for k in range(SSD_CONV):
        for t in range(CHUNK):
            m[k * CHUNK + t, CHUNK + t - (SSD_CONV - 1) + k] = 1.0
    return jnp.asarray(m, dtype=BF16)


def _ssd_sequence(z_ref, x_ref, b_ref, c_ref, dt_ref,
                  cwx_ref, cbx_ref, cwb_ref, cbb_ref, cwc_ref, cbc_ref,
                  dtb_ref, alog_ref, dsk_ref, nw_ref, sel_ref, shf_ref, o_ref,
                  prev_x, prev_b, prev_c, xs_s, bs_s, cs_s, bcast_s, cb_s, ea_s, act_s, dtt_s, st_s, y_s,
                  *, groups, hpg, hdim, nstate):
    L = CHUNK
    gw = hpg * hdim
    assert 2 * hdim == LANES and nstate == LANES and hpg == SUBLANES

    def conv_silu(src_ref, w_ref, bias_ref, prev_ref, lo, width):
        cols = slice(lo, lo + width)
        u = src_ref[:, cols]
        taps = jnp.dot(shf_ref[...], jnp.concatenate([prev_ref[:, cols], u], axis=0),
                       preferred_element_type=F32)
        prev_ref[:, cols] = u
        acc = bias_ref[:, cols] + w_ref[0:1, cols] * taps[0:L]
        for k in range(1, SSD_CONV):
            acc = acc + w_ref[k:k + 1, cols] * taps[k * L:(k + 1) * L]
        return _silu(acc)

    for g in range(groups):
        for j in range(gw // (2 * LANES)):
            cols = slice(j * 2 * LANES, (j + 1) * 2 * LANES)
            xs_s[g, :, cols] = conv_silu(x_ref, cwx_ref, cbx_ref, prev_x,
                                         g * gw + j * 2 * LANES, 2 * LANES)
    for g in range(groups // 2):
        for ref, w_ref, bias_ref, prev_ref, dst in ((b_ref, cwb_ref, cbb_ref, prev_b, bs_s),
                                                    (c_ref, cwc_ref, cbc_ref, prev_c, cs_s)):
            pair = conv_silu(ref, w_ref, bias_ref, prev_ref, 2 * g * nstate, 2 * nstate)
            dst[2 * g] = pair[:, :nstate].astype(BF16)
            dst[2 * g + 1] = pair[:, nstate:].astype(BF16)

    dt_in = dt_ref[...] + dtb_ref[...]
    dt = jnp.maximum(dt_in, 0.0) + jnp.log1p(jnp.exp(-jnp.abs(dt_in)))
    dta = dt * (-jnp.exp(alog_ref[...]))
    row = lax.broadcasted_iota(jnp.int32, (L, LANES), 0)
    lane = lax.broadcasted_iota(jnp.int32, (L, LANES), 1)
    acum = dta
    d = 1
    while d < L:
        acum = acum + jnp.where(row >= d, pltpu.roll(acum, d, 0), 0.0)
        d *= 2
    act_s[...] = acum.T
    dtt_s[...] = dt.T
    dtw = dt * jnp.exp(acum[L - 1:L, :] - acum)
    ea = jnp.exp(acum)
    pieces = _split_bf16_terms(acum) + _split_bf16_terms(dtw)
    for g in range(groups):
        packed = jnp.zeros((L, LANES), F32)
        for t, piece in enumerate(pieces):
            shift = (hpg * t - hpg * g) % LANES
            moved = pltpu.roll(piece, shift, 1) if shift else piece
            packed = jnp.where((lane >= hpg * t) & (lane < hpg * (t + 1)), moved, packed)
        bcast_s[g] = jnp.dot(packed.astype(BF16), sel_ref[...], preferred_element_type=F32)
        cb_s[g] = lax.dot_general(cs_s[g], bs_s[g], (((1,), (1,)), ((), ())),
                                  preferred_element_type=F32)
        shift = (LANES - hpg * g) % LANES
        ea_s[g] = pltpu.roll(ea, shift, 1) if shift else ea

    first_head = lane < hdim
    causal = (lax.broadcasted_iota(jnp.int32, (L, L), 0)
              >= lax.broadcasted_iota(jnp.int32, (L, L), 1))
    n_acol = hpg * LANES

    def group_step(g):
        xs = xs_s[g]
        bc = bs_s[g]
        cc = cs_s[g]
        eag = ea_s[g]
        rows = pl.ds(pl.multiple_of(g * hpg, hpg), hpg)
        at = act_s[rows, :]
        dtt = dtt_s[rows, :]
        st = st_s[g]
        cb = cb_s[g]
        ccf = cc.astype(F32)
        ys, cds = [], []
        for j in range(hpg // 2):
            tile = slice(j * LANES, (j + 1) * LANES)
            lhs_parts = []
            for e in (2 * j, 2 * j + 1):
                acol = bcast_s[g, :, e * LANES:(e + 1) * LANES]
                decay = jnp.exp(jnp.where(causal, acol - at[e:e + 1, :], -jnp.inf))
                lhs_parts.append(jnp.concatenate(
                    [(cb * decay * dtt[e:e + 1, :]).astype(BF16),
                     (ccf * eag[:, e:e + 1]).astype(BF16)], axis=1))
            lhs = jnp.concatenate(lhs_parts, axis=0)
            rhs = jnp.concatenate([xs[:, tile].astype(BF16), st[:, tile].astype(BF16)], axis=0)
            yy = jnp.dot(lhs, rhs, preferred_element_type=F32)
            ys.append(jnp.where(first_head, yy[:L], yy[L:]))
            cds.append(jnp.where(first_head[0:1], eag[L - 1:L, 2 * j:2 * j + 1],
                                 eag[L - 1:L, 2 * j + 1:2 * j + 2]))
        y = jnp.concatenate(ys, axis=1)
        cd = jnp.concatenate(cds, axis=1)
        xdtw = (xs * bcast_s[g, :, n_acol:]).astype(BF16)
        contrib = lax.dot_general(bc, xdtw, (((0,), (0,)), ((), ())),
                                  preferred_element_type=F32)
        st_s[g] = st * cd + contrib
        y_s[g] = y + xs * dsk_ref[g]

    def epilogue():
        for g in range(groups):
            cols = slice(g * gw, (g + 1) * gw)
            y = y_s[g] * _silu(z_ref[:, cols].astype(F32))
            y = y * lax.rsqrt(jnp.mean(y * y, axis=-1, keepdims=True) + NORM_EPS)
            o_ref[:, cols] = (y * nw_ref[:, cols]).astype(o_ref.dtype)

    return group_step, epilogue


def _ssd_kernel(z_all, x_all, b_all, c_all, dt_all, *refs, batch, groups, **dims):
    n_shared = 12
    shared, o_all, scratch = refs[:n_shared], refs[n_shared], refs[n_shared + 1:]
    prev_x, prev_b, prev_c, st_all = scratch[0], scratch[1], scratch[2], scratch[11]

    @pl.when(pl.program_id(0) == 0)
    def _():
        for ref in (prev_x, prev_b, prev_c, st_all):
            ref[...] = jnp.zeros_like(ref)

    sequences = [
        _ssd_sequence(z_all.at[b], x_all.at[b], b_all.at[b], c_all.at[b], dt_all.at[b], *shared,
                      o_all.at[b], *[s.at[b] for s in scratch], groups=groups, **dims)
        for b in range(batch)]

    def group_body(g, carry):
        for group_step, _ in sequences:
            group_step(g)
        return carry

    lax.fori_loop(0, groups, group_body, 0, unroll=2)
    for _, epilogue in sequences:
        epilogue()


def ssd_scan(zx, dt_raw, conv_w, conv_b, dt_bias, a_log, d_skip, norm_w, *, batch, name):
    t = zx.shape[0]
    seq = t // batch
    nc = seq // CHUNK
    groups, nstate, hdim = SSD_GROUPS, SSD_STATE, SSD_HEAD_DIM
    d_inner = norm_w.shape[0]
    heads = d_inner // hdim
    hpg = heads // groups
    gw = hpg * hdim
    gn = groups * nstate
    assert d_inner % gn == 0 and zx.shape[1] == 2 * d_inner + 2 * gn
    assert 2 * _SPLIT_TERMS * hpg <= LANES
    nb = d_inner // gn

    cwx, cwb, cwc = conv_w[:, :d_inner], conv_w[:, d_inner:d_inner + gn], conv_w[:, d_inner + gn:]
    cbx, cbb, cbc = (conv_b[None, :d_inner], conv_b[None, d_inner:d_inner + gn],
                     conv_b[None, d_inner + gn:])
    pad = LANES - heads
    dtb = jnp.pad(dt_bias, (0, pad)).reshape(1, LANES)
    alog = jnp.pad(a_log, (0, pad)).reshape(1, LANES)
    dsk = jnp.repeat(d_skip, hdim).reshape(groups, 1, gw)
    nw = norm_w.reshape(1, d_inner)
    sel = _ssd_selector(hpg, hdim)
    shf = _conv_shift_matrix()

    full = lambda shape: pl.BlockSpec(shape, lambda c: (0,) * len(shape))
    per_seq = lambda *shape: pltpu.VMEM((batch,) + shape[:-1], shape[-1])
    zx3 = zx.reshape(batch, seq, zx.shape[1])
    out = pl.pallas_call(
        functools.partial(_ssd_kernel, batch=batch, groups=groups, hpg=hpg, hdim=hdim,
                          nstate=nstate),
        grid=(nc,),
        in_specs=[pl.BlockSpec((batch, CHUNK, d_inner), lambda c: (0, c, 0)),
                  pl.BlockSpec((batch, CHUNK, d_inner), lambda c: (0, c, 1)),
                  pl.BlockSpec((batch, CHUNK, gn), lambda c: (0, c, 2 * nb)),
                  pl.BlockSpec((batch, CHUNK, gn), lambda c: (0, c, 2 * nb + 1)),
                  pl.BlockSpec((batch, CHUNK, LANES), lambda c: (0, c, 0)),
                  full((SSD_CONV, d_inner)), full((1, d_inner)),
                  full((SSD_CONV, gn)), full((1, gn)),
                  full((SSD_CONV, gn)), full((1, gn)),
                  full((1, LANES)), full((1, LANES)),
                  full((groups, 1, gw)), full((1, d_inner)), full(sel.shape),
                  full(shf.shape)],
        out_specs=pl.BlockSpec((batch, CHUNK, d_inner), lambda c: (0, c, 0)),
        out_shape=jax.ShapeDtypeStruct((batch, seq, d_inner), BF16),
        scratch_shapes=[per_seq(CHUNK, d_inner, BF16),
                        per_seq(CHUNK, gn, BF16),
                        per_seq(CHUNK, gn, BF16),
                        per_seq(groups, CHUNK, gw, F32),
                        per_seq(groups, CHUNK, nstate, BF16),
                        per_seq(groups, CHUNK, nstate, BF16),
                        per_seq(groups, CHUNK, sel.shape[1], F32),
                        per_seq(groups, CHUNK, CHUNK, F32),
                        per_seq(groups, CHUNK, LANES, F32),
                        per_seq(LANES, CHUNK, F32),
                        per_seq(LANES, CHUNK, F32),
                        per_seq(groups, nstate, gw, F32),
                        per_seq(groups, CHUNK, gw, F32)],
        compiler_params=_params(("arbitrary",)),
        name=name,
    )(zx3, zx3, zx3, zx3, dt_raw.reshape(batch, seq, LANES), cwx, cbx, cwb, cbb, cwc, cbc,
      dtb, alog, dsk, nw, sel, shf)
    return out.reshape(t, d_inner)


def _ret_kernel(q_ref, k_ref, v_ref, g_ref, cos_ref, sin_ref,
                idec_ref, qdec_ref, kdec_ref, cdec_ref, o_ref, st_s,
                *, batch, heads, dk, dv):
    @pl.when(pl.program_id(0) == 0)
    def _():
        st_s[...] = jnp.zeros_like(st_s)

    cos = cos_ref[...]
    sin = sin_ref[...]
    half = dk // 2

    def rope(ref, b, h):
        lo = ref[b, :, h * dk:h * dk + half].astype(F32)
        hi = ref[b, :, h * dk + half:(h + 1) * dk].astype(F32)
        return jnp.concatenate([lo * cos - hi * sin, hi * cos + lo * sin], axis=1)

    for h in range(heads):
        for b in range(batch):
            qr = rope(q_ref, b, h)
            kr = rope(k_ref, b, h) * dk ** -0.5
            v = v_ref[b, :, h * dv:(h + 1) * dv]
            s = lax.dot_general(qr.astype(BF16), kr.astype(BF16), (((1,), (1,)), ((), ())),
                                preferred_element_type=F32) * idec_ref[h]
            intra = jnp.dot(s.astype(BF16), v, preferred_element_type=F32)
            st = st_s[b, h]
            cross = jnp.dot((qr * qdec_ref[h]).astype(BF16), st.astype(BF16),
                            preferred_element_type=F32)
            st_s[b, h] = st * cdec_ref[h] + lax.dot_general(
                (kr * kdec_ref[h]).astype(BF16), v, (((0,), (0,)), ((), ())),
                preferred_element_type=F32)
            y = intra + cross
            y = y * lax.rsqrt(jnp.mean(y * y, axis=-1, keepdims=True) + NORM_EPS)
            gate = g_ref[b, :, h * dv:(h + 1) * dv].astype(F32)
            o_ref[b, :, h * dv:(h + 1) * dv] = (y * _silu(gate)).astype(o_ref.dtype)


def retention_scan(proj, *, batch, qk_dim, v_dim, name):
    t = proj.shape[0]
    seq = t // batch
    nc = seq // CHUNK
    heads = RET_HEADS
    dk, dv = qk_dim // heads, v_dim // heads
    assert v_dim == 2 * qk_dim and proj.shape[1] == 2 * qk_dim + 2 * v_dim

    pos = jnp.arange(seq, dtype=F32)
    inv_freq = 1.0 / (ROPE_BASE ** jnp.linspace(0.0, 1.0, dk // 2, dtype=F32))
    theta = pos[:, None] * inv_freq[None, :]
    cos, sin = jnp.cos(theta), jnp.sin(theta)
    log_gamma = jnp.log1p(-jnp.exp2(-5.0 - jnp.arange(heads, dtype=F32)))
    idx = jnp.arange(CHUNK, dtype=F32)
    causal = jnp.tril(jnp.ones((CHUNK, CHUNK), dtype=bool))
    idec = jnp.exp(jnp.where(causal[None],
                             (idx[:, None] - idx[None, :])[None] * log_gamma[:, None, None],
                             -jnp.inf))
    qdec = jnp.exp(log_gamma[:, None] * (idx + 1.0)[None, :])[..., None]
    kdec = jnp.exp(log_gamma[:, None] * (CHUNK - 1.0 - idx)[None, :])[..., None]
    cdec = jnp.exp(CHUNK * log_gamma)[:, None, None]

    full = lambda shape: pl.BlockSpec(shape, lambda c: (0,) * len(shape))
    proj3 = proj.reshape(batch, seq, proj.shape[1])
    out = pl.pallas_call(
        functools.partial(_ret_kernel, batch=batch, heads=heads, dk=dk, dv=dv),
        grid=(nc,),
        in_specs=[pl.BlockSpec((batch, CHUNK, qk_dim), lambda c: (0, c, 0)),
                  pl.BlockSpec((batch, CHUNK, qk_dim), lambda c: (0, c, 1)),
                  pl.BlockSpec((batch, CHUNK, v_dim), lambda c: (0, c, 1)),
                  pl.BlockSpec((batch, CHUNK, v_dim), lambda c: (0, c, 2)),
                  pl.BlockSpec((CHUNK, dk // 2), lambda c: (c, 0)),
                  pl.BlockSpec((CHUNK, dk // 2), lambda c: (c, 0)),
                  full((heads, CHUNK, CHUNK)), full((heads, CHUNK, 1)),
                  full((heads, CHUNK, 1)), full((heads, 1, 1))],
        out_specs=pl.BlockSpec((batch, CHUNK, v_dim), lambda c: (0, c, 0)),
        out_shape=jax.ShapeDtypeStruct((batch, seq, v_dim), BF16),
        scratch_shapes=[pltpu.VMEM((batch, heads, dk, dv), F32)],
        compiler_params=_params(("arbitrary",)),
        name=name,
    )(proj3, proj3, proj3, proj3, cos, sin, idec, qdec, kdec, cdec)
    return out.reshape(t, v_dim)


def _deinterleave_kernel(w_ref, perm_ref, o_ref, *, n_perm, head_dim):
    @pl.when(pl.program_id(0) < n_perm)
    def _():
        for lo in range(0, o_ref.shape[1], head_dim):
            cols = slice(lo, lo + head_dim)
            o_ref[:, cols] = jnp.dot(w_ref[:, cols].astype(BF16), perm_ref[...],
                                     preferred_element_type=F32).astype(o_ref.dtype)

    @pl.when(pl.program_id(0) >= n_perm)
    def _():
        o_ref[...] = w_ref[...].astype(o_ref.dtype)


def deinterleave_qk_columns(w, layer, *, qk_cols, head_dim, tn, name):
    k, n = w.shape[1], w.shape[2]
    assert qk_cols % tn == 0 and tn % head_dim == 0
    src = np.arange(head_dim).reshape(head_dim // 2, 2).T.reshape(-1)
    perm = jnp.asarray(np.eye(head_dim, dtype=np.float32)[:, src], dtype=BF16)
    return pl.pallas_call(
        functools.partial(_deinterleave_kernel, n_perm=qk_cols // tn, head_dim=head_dim),
        grid=(n // tn,),
        in_specs=[pl.BlockSpec((None, k, tn), lambda j: (layer, 0, j)),
                  pl.BlockSpec((head_dim, head_dim), lambda j: (0, 0))],
        out_specs=pl.BlockSpec((k, tn), lambda j: (0, j)),
        out_shape=jax.ShapeDtypeStruct((k, n), BF16),
        compiler_params=_params(("arbitrary",)),
        name=name,
    )(w, perm)


def _cast_kernel(w_ref, o_ref):
    o_ref[...] = w_ref[...].astype(o_ref.dtype)


def cast_columns_bf16(w, *, col_start, n_cols, tn, name):
    layers, k, _ = w.shape
    assert col_start % tn == 0 and n_cols % tn == 0
    first = col_start // tn
    return pl.pallas_call(
        _cast_kernel,
        grid=(layers, n_cols // tn),
        in_specs=[pl.BlockSpec((None, k, tn), lambda l, j: (l, 0, j + first))],
        out_specs=pl.BlockSpec((None, k, tn), lambda l, j: (l, 0, j)),
        out_shape=jax.ShapeDtypeStruct((layers, k, n_cols), BF16),
        compiler_params=_params(("arbitrary", "arbitrary")),
        name=name,
    )(w)


def kernel(x, mem, ssd_in_w, ssd_conv_w, ssd_conv_b, ssd_dt_bias, ssd_a_log, ssd_d, ssd_norm_w,
           ssd_out_w, ret_in_w, ret_out_w, mixer_norm_w, mem_q_norm_w, mem_kv_norm_w, mem_q_w,
           mem_kv_w, mem_o_w, ffn_norm_w, ffn_in_w, ffn_out_w, final_norm_w):
    batch, seq, d = x.shape
    depth = mixer_norm_w.shape[0]
    xf = x.reshape(batch * seq, d)
    memf = mem.reshape(-1, d)
    ssd_inner = ssd_norm_w.shape[1]
    ssd_main = 2 * ssd_inner + 2 * SSD_GROUPS * SSD_STATE
    ssd_heads = ssd_inner // SSD_HEAD_DIM
    ret_qk = ret_in_w.shape[2] // 6
    ret_v = 2 * ret_qk

    ssd_in_b = cast_columns_bf16(ssd_in_w, col_start=0, n_cols=ssd_main, tn=1024,
                                 name="cast_ssd_in")
    ssd_out_b = ssd_out_w.astype(BF16)
    ssd_dt_b = jnp.pad(ssd_in_w[:, :, ssd_main:],
                       ((0, 0), (0, 0), (0, LANES - ssd_heads))).astype(BF16)
    ret_out_b = ret_out_w.astype(BF16)
    mem_q_b, mem_o_b = mem_q_w.astype(BF16), mem_o_w.astype(BF16)
    ffn_hidden = ffn_out_w.shape[1]
    ffn_gate_b = cast_columns_bf16(ffn_in_w, col_start=0, n_cols=ffn_hidden, tn=512,
                                   name="cast_ffn_gate")
    ffn_up_b = cast_columns_bf16(ffn_in_w, col_start=ffn_hidden, n_cols=ffn_hidden, tn=512,
                                 name="cast_ffn_up")
    ffn_out_b = ffn_out_w.astype(BF16)

    for i in range(depth):
        j = i // 2
        if i % 2 == 0:
            w_dt = ssd_dt_b[j]
            zx, dt_raw = norm_matmul_extra(xf, mixer_norm_w[i], ssd_in_b, j, w_dt, n_out=ssd_main,
                                           tm=1024, tn=2048, out_dtype=BF16, name=f"ssd_in_{i}")
            y = ssd_scan(zx, dt_raw, ssd_conv_w[j], ssd_conv_b[j], ssd_dt_bias[j], ssd_a_log[j],
                         ssd_d[j], ssd_norm_w[j], batch=batch, name=f"ssd_scan_{i}")
            xf = matmul_residual(y, ssd_out_b, j, xf, tm=1024, tn=1024, tk=y.shape[1],
                                 name=f"ssd_out_{i}")
        else:
            w_in = deinterleave_qk_columns(ret_in_w, j, qk_cols=2 * ret_qk,
                                           head_dim=ret_qk // RET_HEADS, tn=1024,
                                           name=f"ret_perm_{i}")
            proj = norm_matmul(xf, mixer_norm_w[i], w_in[None], 0, n_out=w_in.shape[1], tm=1024,
                               tn=2048, out_dtype=BF16, name=f"ret_in_{i}")
            y = retention_scan(proj, batch=batch, qk_dim=ret_qk, v_dim=ret_v,
                               name=f"ret_scan_{i}")
            xf = matmul_residual(y, ret_out_b, j, xf, tm=1024, tn=1024, tk=y.shape[1],
                                 name=f"ret_out_{i}")

        kv = norm_matmul(memf, mem_kv_norm_w[i], mem_kv_w, i, n_out=2 * d, tm=512, tn=1024,
                         out_dtype=BF16, name=f"mem_kv_{i}")
        xf = memory_block(xf, mem_q_norm_w[i], mem_q_b, mem_o_b, i, kv, batch=batch,
                          heads=MEM_HEADS, tm=512, name=f"mem_blk_{i}")

        hid = norm_glu(xf, ffn_norm_w[i], ffn_gate_b, ffn_up_b, i, tm=1024, tn=512,
                       name=f"ffn_in_{i}")
        if i < depth - 1:
            xf = matmul_residual(hid, ffn_out_b, i, xf, tm=1024, tn=1024,
                                 tk=hid.shape[1] // 2, name=f"ffn_out_{i}")
        else:
            xf = matmul_residual_norm(hid, ffn_out_b, i, xf, final_norm_w, tm=512,
                                      tk=hid.shape[1] // 2, name=f"ffn_out_{i}")

    return xf.reshape(batch, seq, d)
```

```python
import functools

import numpy as np
import jax
import jax.numpy as jnp
from jax import lax
from jax.experimental import pallas as pl
from jax.experimental.pallas import tpu as pltpu

NORM_EPS = 1e-6
CHUNK = 128
LANES = 128
SUBLANES = 8
VMEM_LIMIT_BYTES = 56 * 1024 * 1024

SSD_HEAD_DIM = 64
SSD_GROUPS = 8
SSD_STATE = 128
SSD_CONV = 4
RET_HEADS = 8
MEM_HEADS = 4
ROPE_BASE = 10000.0

F32 = jnp.float32
BF16 = jnp.bfloat16


def _params(semantics):
    return pltpu.CompilerParams(dimension_semantics=semantics,
                                vmem_limit_bytes=VMEM_LIMIT_BYTES)


def _silu(v):
    h = 0.5 * v
    return h + h * jnp.tanh(h)


def _rms_rows(x, w):
    ms = jnp.mean(x * x, axis=-1, keepdims=True)
    return x * lax.rsqrt(ms + NORM_EPS) * w


def _norm_prologue(x_ref, nw_ref, h_ref):
    @pl.when(pl.program_id(1) == 0)
    def _():
        h_ref[...] = _rms_rows(x_ref[...], nw_ref[...]).astype(h_ref.dtype)


def _norm_mm_kernel(x_ref, nw_ref, w_ref, o_ref, h_ref):
    _norm_prologue(x_ref, nw_ref, h_ref)
    o_ref[...] = jnp.dot(h_ref[...], w_ref[...].astype(BF16),
                         preferred_element_type=F32).astype(o_ref.dtype)


def norm_matmul(x, nw, w, layer, *, n_out, tm, tn, out_dtype, name):
    m, k = x.shape
    tm = min(tm, m)
    return pl.pallas_call(
        _norm_mm_kernel,
        grid=(m // tm, n_out // tn),
        in_specs=[pl.BlockSpec((tm, k), lambda i, j: (i, 0)),
                  pl.BlockSpec((1, k), lambda i, j: (0, 0)),
                  pl.BlockSpec((None, k, tn), lambda i, j: (layer, 0, j))],
        out_specs=pl.BlockSpec((tm, tn), lambda i, j: (i, j)),
        out_shape=jax.ShapeDtypeStruct((m, n_out), out_dtype),
        scratch_shapes=[pltpu.VMEM((tm, k), BF16)],
        compiler_params=_params(("arbitrary", "arbitrary")),
        name=name,
    )(x, nw.reshape(1, k), w)


def _norm_mm_extra_kernel(x_ref, nw_ref, w_ref, we_ref, o_ref, oe_ref, h_ref):
    _norm_prologue(x_ref, nw_ref, h_ref)
    o_ref[...] = jnp.dot(h_ref[...], w_ref[...],
                         preferred_element_type=F32).astype(o_ref.dtype)

    @pl.when(pl.program_id(1) == pl.num_programs(1) - 1)
    def _():
        oe_ref[...] = jnp.dot(h_ref[...], we_ref[...], preferred_element_type=F32)


def norm_matmul_extra(x, nw, w, layer, w_extra, *, n_out, tm, tn, out_dtype, name):
    m, k = x.shape
    ne = w_extra.shape[1]
    tm = min(tm, m)
    return pl.pallas_call(
        _norm_mm_extra_kernel,
        grid=(m // tm, n_out // tn),
        in_specs=[pl.BlockSpec((tm, k), lambda i, j: (i, 0)),
                  pl.BlockSpec((1, k), lambda i, j: (0, 0)),
                  pl.BlockSpec((None, k, tn), lambda i, j: (layer, 0, j)),
                  pl.BlockSpec((k, ne), lambda i, j: (0, 0))],
        out_specs=[pl.BlockSpec((tm, tn), lambda i, j: (i, j)),
                   pl.BlockSpec((tm, ne), lambda i, j: (i, 0))],
        out_shape=[jax.ShapeDtypeStruct((m, n_out), out_dtype),
                   jax.ShapeDtypeStruct((m, ne), F32)],
        scratch_shapes=[pltpu.VMEM((tm, k), BF16)],
        compiler_params=_params(("arbitrary", "arbitrary")),
        name=name,
    )(x, nw.reshape(1, k), w, w_extra)


def _norm_glu_kernel(x_ref, nw_ref, wg_ref, wu_ref, o_ref, h_ref):
    _norm_prologue(x_ref, nw_ref, h_ref)
    h = h_ref[...]
    g = jnp.dot(h, wg_ref[...], preferred_element_type=F32)
    u = jnp.dot(h, wu_ref[...], preferred_element_type=F32)
    o_ref[...] = (_silu(g) * u).astype(o_ref.dtype)


def norm_glu(x, nw, w_gate, w_up, layer, *, tm, tn, name):
    m, k = x.shape
    hidden = w_gate.shape[2]
    tm = min(tm, m)
    w_spec = pl.BlockSpec((None, k, tn), lambda i, j: (layer, 0, j))
    return pl.pallas_call(
        _norm_glu_kernel,
        grid=(m // tm, hidden // tn),
        in_specs=[pl.BlockSpec((tm, k), lambda i, j: (i, 0)),
                  pl.BlockSpec((1, k), lambda i, j: (0, 0)),
                  w_spec, w_spec],
        out_specs=pl.BlockSpec((tm, tn), lambda i, j: (i, j)),
        out_shape=jax.ShapeDtypeStruct((m, hidden), BF16),
        scratch_shapes=[pltpu.VMEM((tm, k), BF16)],
        compiler_params=_params(("arbitrary", "arbitrary")),
        name=name,
    )(x, nw.reshape(1, k), w_gate, w_up)


def _mm_res_kernel(a_ref, w_ref, r_ref, o_ref, *, nk):
    p = jnp.dot(a_ref[...], w_ref[...], preferred_element_type=F32)
    if nk == 1:
        o_ref[...] = r_ref[...] + p
    else:
        k = pl.program_id(2)

        @pl.when(k == 0)
        def _():
            o_ref[...] = r_ref[...] + p

        @pl.when(k > 0)
        def _():
            o_ref[...] += p


def matmul_residual(a, w, layer, res, *, tm, tn, tk, name):
    m, k = a.shape
    n = w.shape[2]
    tm = min(tm, m)
    nk = k // tk
    return pl.pallas_call(
        functools.partial(_mm_res_kernel, nk=nk),
        grid=(m // tm, n // tn, nk),
        in_specs=[pl.BlockSpec((tm, tk), lambda i, j, kk: (i, kk)),
                  pl.BlockSpec((None, tk, tn), lambda i, j, kk: (layer, kk, j)),
                  pl.BlockSpec((tm, tn), lambda i, j, kk: (i, j))],
        out_specs=pl.BlockSpec((tm, tn), lambda i, j, kk: (i, j)),
        out_shape=jax.ShapeDtypeStruct((m, n), F32),
        compiler_params=_params(("arbitrary", "arbitrary", "arbitrary")),
        name=name,
    )(a, w, res)


def _mm_res_norm_kernel(a_ref, w_ref, r_ref, nw_ref, o_ref, *, nk):
    p = jnp.dot(a_ref[...], w_ref[...], preferred_element_type=F32)
    k = pl.program_id(1)
    if nk == 1:
        o_ref[...] = _rms_rows(r_ref[...] + p, nw_ref[...])
    else:
        @pl.when(k == 0)
        def _():
            o_ref[...] = r_ref[...] + p

        @pl.when((k > 0) & (k < nk - 1))
        def _():
            o_ref[...] += p

        @pl.when(k == nk - 1)
        def _():
            o_ref[...] = _rms_rows(o_ref[...] + p, nw_ref[...])


def matmul_residual_norm(a, w, layer, res, nw, *, tm, tk, name):
    m, k = a.shape
    n = w.shape[2]
    tm = min(tm, m)
    nk = k // tk
    return pl.pallas_call(
        functools.partial(_mm_res_norm_kernel, nk=nk),
        grid=(m // tm, nk),
        in_specs=[pl.BlockSpec((tm, tk), lambda i, kk: (i, kk)),
                  pl.BlockSpec((None, tk, n), lambda i, kk: (layer, kk, 0)),
                  pl.BlockSpec((tm, n), lambda i, kk: (i, 0)),
                  pl.BlockSpec((1, n), lambda i, kk: (0, 0))],
        out_specs=pl.BlockSpec((tm, n), lambda i, kk: (i, 0)),
        out_shape=jax.ShapeDtypeStruct((m, n), F32),
        compiler_params=_params(("arbitrary", "arbitrary")),
        name=name,
    )(a, w, res, nw.reshape(1, n))


def _mem_block_kernel(x_ref, nw_ref, wq_ref, k_ref, v_ref, wo_ref, o_ref, *, heads, hd):
    scale = hd ** -0.5
    x = x_ref[...]
    h = _rms_rows(x, nw_ref[...]).astype(BF16)
    q = jnp.dot(h, wq_ref[...], preferred_element_type=F32).astype(BF16)
    outs = []
    for hh in range(heads):
        sl = slice(hh * hd, (hh + 1) * hd)
        s = lax.dot_general(q[:, sl], k_ref[:, sl], (((1,), (1,)), ((), ())),
                            preferred_element_type=F32) * scale
        e = jnp.exp(s - jnp.max(s, axis=-1, keepdims=True))
        p = e / jnp.sum(e, axis=-1, keepdims=True)
        outs.append(jnp.dot(p.astype(BF16), v_ref[:, sl],
                            preferred_element_type=F32).astype(BF16))
    o = jnp.concatenate(outs, axis=1)
    o_ref[...] = x + jnp.dot(o, wo_ref[...], preferred_element_type=F32)


def memory_block(x, nw, wq, wo, layer, kv, *, batch, heads, tm, name):
    t, d = x.shape
    seq = t // batch
    mtok = kv.shape[0] // batch
    tm = min(tm, seq)
    per_batch = seq // tm
    resident = lambda: pl.BlockSpec((None, d, d), lambda i: (layer, 0, 0),
                                    pipeline_mode=pl.Buffered(1))
    return pl.pallas_call(
        functools.partial(_mem_block_kernel, heads=heads, hd=d // heads),
        grid=(t // tm,),
        in_specs=[pl.BlockSpec((tm, d), lambda i: (i, 0)),
                  pl.BlockSpec((1, d), lambda i: (0, 0)),
                  resident(),
                  pl.BlockSpec((mtok, d), lambda i: (i // per_batch, 0)),
                  pl.BlockSpec((mtok, d), lambda i: (i // per_batch, 1)),
                  resident()],
        out_specs=pl.BlockSpec((tm, d), lambda i: (i, 0)),
        out_shape=jax.ShapeDtypeStruct((t, d), F32),
        compiler_params=_params(("arbitrary",)),
        name=name,
    )(x, nw.reshape(1, d), wq, kv, kv, wo)


_SPLIT_TERMS = 3


def _split_bf16_terms(v):
    terms = []
    r = v
    for _ in range(_SPLIT_TERMS):
        t = r.astype(BF16).astype(F32)
        terms.append(t)
        r = r - t
    return terms


def _ssd_selector(hpg, hdim):
    sel = np.zeros((LANES, hpg * LANES + hpg * hdim), np.float32)
    for t in range(_SPLIT_TERMS):
        for e in range(hpg):
            sel[t * hpg + e, e * LANES:(e + 1) * LANES] = 1.0
            base = hpg * LANES
            sel[(_SPLIT_TERMS + t) * hpg + e, base + e * hdim:base + (e + 1) * hdim] = 1.0
    return jnp.asarray(sel, dtype=BF16)


def _conv_shift_matrix():
    m = np.zeros((SSD_CONV * CHUNK, 2 * CHUNK), np.float32)
    for k in range(SSD_CONV):
        for t in range(CHUNK):
            m[k * CHUNK + t, CHUNK + t - (SSD_CONV - 1) + k] = 1.0
    return jnp.asarray(m, dtype=BF16)


def _ssd_sequence(z_ref, x_ref, b_ref, c_ref, dt_ref,
                  cwx_ref, cbx_ref, cwb_ref, cbb_ref, cwc_ref, cbc_ref,
                  dtb_ref, alog_ref, dsk_ref, nw_ref, sel_ref, shf_ref, o_ref,
                  prev_x, prev_b, prev_c, xs_s, bs_s, cs_s, bcast_s, cb_s, ea_s, act_s, dtt_s, st_s, y_s,
                  *, groups, hpg, hdim, nstate):
    L = CHUNK
    gw = hpg * hdim
    assert 2 * hdim == LANES and nstate == LANES and hpg == SUBLANES

    def conv_silu(src_ref, w_ref, bias_ref, prev_ref, lo, width):
        cols = slice(lo, lo + width)
        u = src_ref[:, cols]
        taps = jnp.dot(shf_ref[...], jnp.concatenate([prev_ref[:, cols], u], axis=0),
                       preferred_element_type=F32)
        prev_ref[:, cols] = u
        acc = bias_ref[:, cols] + w_ref[0:1, cols] * taps[0:L]
        for k in range(1, SSD_CONV):
            acc = acc + w_ref[k:k + 1, cols] * taps[k * L:(k + 1) * L]
        return _silu(acc)

    for g in range(groups):
        for j in range(gw // (2 * LANES)):
            cols = slice(j * 2 * LANES, (j + 1) * 2 * LANES)
            xs_s[g, :, cols] = conv_silu(x_ref, cwx_ref, cbx_ref, prev_x,
                                         g * gw + j * 2 * LANES, 2 * LANES)
    for g in range(groups // 2):
        for ref, w_ref, bias_ref, prev_ref, dst in ((b_ref, cwb_ref, cbb_ref, prev_b, bs_s),
                                                    (c_ref, cwc_ref, cbc_ref, prev_c, cs_s)):
            pair = conv_silu(ref, w_ref, bias_ref, prev_ref, 2 * g * nstate, 2 * nstate)
            dst[2 * g] = pair[:, :nstate].astype(BF16)
            dst[2 * g + 1] = pair[:, nstate:].astype(BF16)

    dt_in = dt_ref[...] + dtb_ref[...]
    dt = jnp.maximum(dt_in, 0.0) + jnp.log1p(jnp.exp(-jnp.abs(dt_in)))
    dta = dt * (-jnp.exp(alog_ref[...]))
    row = lax.broadcasted_iota(jnp.int32, (L, LANES), 0)
    lane = lax.broadcasted_iota(jnp.int32, (L, LANES), 1)
    acum = dta
    d = 1
    while d < L:
        acum = acum + jnp.where(row >= d, pltpu.roll(acum, d, 0), 0.0)
        d *= 2
    act_s[...] = acum.T
    dtt_s[...] = dt.T
    dtw = dt * jnp.exp(acum[L - 1:L, :] - acum)
    ea = jnp.exp(acum)
    pieces = _split_bf16_terms(acum) + _split_bf16_terms(dtw)
    for g in range(groups):
        packed = jnp.zeros((L, LANES), F32)
        for t, piece in enumerate(pieces):
            shift = (hpg * t - hpg * g) % LANES
            moved = pltpu.roll(piece, shift, 1) if shift else piece
            packed = jnp.where((lane >= hpg * t) & (lane < hpg * (t + 1)), moved, packed)
        bcast_s[g] = jnp.dot(packed.astype(BF16), sel_ref[...], preferred_element_type=F32)
        cb_s[g] = lax.dot_general(cs_s[g], bs_s[g], (((1,), (1,)), ((), ())),
                                  preferred_element_type=F32)
        shift = (LANES - hpg * g) % LANES
        ea_s[g] = pltpu.roll(ea, shift, 1) if shift else ea

    first_head = lane < hdim
    causal = (lax.broadcasted_iota(jnp.int32, (L, L), 0)
              >= lax.broadcasted_iota(jnp.int32, (L, L), 1))
    n_acol = hpg * LANES

    def group_step(g):
        xs = xs_s[g]
        bc = bs_s[g]
        cc = cs_s[g]
        eag = ea_s[g]
        rows = pl.ds(pl.multiple_of(g * hpg, hpg), hpg)
        at = act_s[rows, :]
        dtt = dtt_s[rows, :]
        st = st_s[g]
        cb = cb_s[g]
        ccf = cc.astype(F32)
        ys, cds = [], []
        for j in range(hpg // 2):
            tile = slice(j * LANES, (j + 1) * LANES)
            lhs_parts = []
            for e in (2 * j, 2 * j + 1):
                acol = bcast_s[g, :, e * LANES:(e + 1) * LANES]
                decay = jnp.exp(jnp.where(causal, acol - at[e:e + 1, :], -jnp.inf))
                lhs_parts.append(jnp.concatenate(
                    [(cb * decay * dtt[e:e + 1, :]).astype(BF16),
                     (ccf * eag[:, e:e + 1]).astype(BF16)], axis=1))
            lhs = jnp.concatenate(lhs_parts, axis=0)
            rhs = jnp.concatenate([xs[:, tile].astype(BF16), st[:, tile].astype(BF16)], axis=0)
            yy = jnp.dot(lhs, rhs, preferred_element_type=F32)
            ys.append(jnp.where(first_head, yy[:L], yy[L:]))
            cds.append(jnp.where(first_head[0:1], eag[L - 1:L, 2 * j:2 * j + 1],
                                 eag[L - 1:L, 2 * j + 1:2 * j + 2]))
        y = jnp.concatenate(ys, axis=1)
        cd = jnp.concatenate(cds, axis=1)
        xdtw = (xs * bcast_s[g, :, n_acol:]).astype(BF16)
        contrib = lax.dot_general(bc, xdtw, (((0,), (0,)), ((), ())),
                                  preferred_element_type=F32)
        st_s[g] = st * cd + contrib
        y_s[g] = y + xs * dsk_ref[g]

    def epilogue():
        for g in range(groups):
            cols = slice(g * gw, (g + 1) * gw)
            y = y_s[g] * _silu(z_ref[:, cols].astype(F32))
            y = y * lax.rsqrt(jnp.mean(y * y, axis=-1, keepdims=True) + NORM_EPS)
            o_ref[:, cols] = (y * nw_ref[:, cols]).astype(o_ref.dtype)

    return group_step, epilogue


def _ssd_kernel(z_all, x_all, b_all, c_all, dt_all, *refs, batch, groups, **dims):
    n_shared = 12
    shared, o_all, scratch = refs[:n_shared], refs[n_shared], refs[n_shared + 1:]
    prev_x, prev_b, prev_c, st_all = scratch[0], scratch[1], scratch[2], scratch[11]

    @pl.when(pl.program_id(0) == 0)
    def _():
        for ref in (prev_x, prev_b, prev_c, st_all):
            ref[...] = jnp.zeros_like(ref)

    sequences = [
        _ssd_sequence(z_all.at[b], x_all.at[b], b_all.at[b], c_all.at[b], dt_all.at[b], *shared,
                      o_all.at[b], *[s.at[b] for s in scratch], groups=groups, **dims)
        for b in range(batch)]

    def group_body(g, carry):
        for group_step, _ in sequences:
            group_step(g)
        return carry

    lax.fori_loop(0, groups, group_body, 0, unroll=2)
    for _, epilogue in sequences:
        epilogue()


def ssd_scan(zx, dt_raw, conv_w, conv_b, dt_bias, a_log, d_skip, norm_w, *, batch, name):
    t = zx.shape[0]
    seq = t // batch
    nc = seq // CHUNK
    groups, nstate, hdim = SSD_GROUPS, SSD_STATE, SSD_HEAD_DIM
    d_inner = norm_w.shape[0]
    heads = d_inner // hdim
    hpg = heads // groups
    gw = hpg * hdim
    gn = groups * nstate
    assert d_inner % gn == 0 and zx.shape[1] == 2 * d_inner + 2 * gn
    assert 2 * _SPLIT_TERMS * hpg <= LANES
    nb = d_inner // gn

    cwx, cwb, cwc = conv_w[:, :d_inner], conv_w[:, d_inner:d_inner + gn], conv_w[:, d_inner + gn:]
    cbx, cbb, cbc = (conv_b[None, :d_inner], conv_b[None, d_inner:d_inner + gn],
                     conv_b[None, d_inner + gn:])
    pad = LANES - heads
    dtb = jnp.pad(dt_bias, (0, pad)).reshape(1, LANES)
    alog = jnp.pad(a_log, (0, pad)).reshape(1, LANES)
    dsk = jnp.repeat(d_skip, hdim).reshape(groups, 1, gw)
    nw = norm_w.reshape(1, d_inner)
    sel = _ssd_selector(hpg, hdim)
    shf = _conv_shift_matrix()

    full = lambda shape: pl.BlockSpec(shape, lambda c: (0,) * len(shape))
    per_seq = lambda *shape: pltpu.VMEM((batch,) + shape[:-1], shape[-1])
    zx3 = zx.reshape(batch, seq, zx.shape[1])
    out = pl.pallas_call(
        functools.partial(_ssd_kernel, batch=batch, groups=groups, hpg=hpg, hdim=hdim,
                          nstate=nstate),
        grid=(nc,),
        in_specs=[pl.BlockSpec((batch, CHUNK, d_inner), lambda c: (0, c, 0)),
                  pl.BlockSpec((batch, CHUNK, d_inner), lambda c: (0, c, 1)),
                  pl.BlockSpec((batch, CHUNK, gn), lambda c: (0, c, 2 * nb)),
                  pl.BlockSpec((batch, CHUNK, gn), lambda c: (0, c, 2 * nb + 1)),
                  pl.BlockSpec((batch, CHUNK, LANES), lambda c: (0, c, 0)),
                  full((SSD_CONV, d_inner)), full((1, d_inner)),
                  full((SSD_CONV, gn)), full((1, gn)),
                  full((SSD_CONV, gn)), full((1, gn)),
                  full((1, LANES)), full((1, LANES)),
                  full((groups, 1, gw)), full((1, d_inner)), full(sel.shape),
                  full(shf.shape)],
        out_specs=pl.BlockSpec((batch, CHUNK, d_inner), lambda c: (0, c, 0)),
        out_shape=jax.ShapeDtypeStruct((batch, seq, d_inner), BF16),
        scratch_shapes=[per_seq(CHUNK, d_inner, BF16),
                        per_seq(CHUNK, gn, BF16),
                        per_seq(CHUNK, gn, BF16),
                        per_seq(groups, CHUNK, gw, F32),
                        per_seq(groups, CHUNK, nstate, BF16),
                        per_seq(groups, CHUNK, nstate, BF16),
                        per_seq(groups, CHUNK, sel.shape[1], F32),
                        per_seq(groups, CHUNK, CHUNK, F32),
                        per_seq(groups, CHUNK, LANES, F32),
                        per_seq(LANES, CHUNK, F32),
                        per_seq(LANES, CHUNK, F32),
                        per_seq(groups, nstate, gw, F32),
                        per_seq(groups, CHUNK, gw, F32)],
        compiler_params=_params(("arbitrary",)),
        name=name,
    )(zx3, zx3, zx3, zx3, dt_raw.reshape(batch, seq, LANES), cwx, cbx, cwb, cbb, cwc, cbc,
      dtb, alog, dsk, nw, sel, shf)
    return out.reshape(t, d_inner)


def _ret_kernel(q_ref, k_ref, v_ref, g_ref, cos_ref, sin_ref,
                idec_ref, qdec_ref, kdec_ref, cdec_ref, o_ref, st_s,
                *, batch, heads, dk, dv):
    @pl.when(pl.program_id(0) == 0)
    def _():
        st_s[...] = jnp.zeros_like(st_s)

    cos = cos_ref[...]
    sin = sin_ref[...]
    half = dk // 2

    def rope(ref, b, h):
        lo = ref[b, :, h * dk:h * dk + half].astype(F32)
        hi = ref[b, :, h * dk + half:(h + 1) * dk].astype(F32)
        return jnp.concatenate([lo * cos - hi * sin, hi * cos + lo * sin], axis=1)

    for h in range(heads):
        for b in range(batch):
            qr = rope(q_ref, b, h)
            kr = rope(k_ref, b, h) * dk ** -0.5
            v = v_ref[b, :, h * dv:(h + 1) * dv]
            s = lax.dot_general(qr.astype(BF16), kr.astype(BF16), (((1,), (1,)), ((), ())),
                                preferred_element_type=F32) * idec_ref[h]
            intra = jnp.dot(s.astype(BF16), v, preferred_element_type=F32)
            st = st_s[b, h]
            cross = jnp.dot((qr * qdec_ref[h]).astype(BF16), st.astype(BF16),
                            preferred_element_type=F32)
            st_s[b, h] = st * cdec_ref[h] + lax.dot_general(
                (kr * kdec_ref[h]).astype(BF16), v, (((0,), (0,)), ((), ())),
                preferred_element_type=F32)
            y = intra + cross
            y = y * lax.rsqrt(jnp.mean(y * y, axis=-1, keepdims=True) + NORM_EPS)
            gate = g_ref[b, :, h * dv:(h + 1) * dv].astype(F32)
            o_ref[b, :, h * dv:(h + 1) * dv] = (y * _silu(gate)).astype(o_ref.dtype)


def retention_scan(proj, *, batch, qk_dim, v_dim, name):
    t = proj.shape[0]
    seq = t // batch
    nc = seq // CHUNK
    heads = RET_HEADS
    dk, dv = qk_dim // heads, v_dim // heads
    assert v_dim == 2 * qk_dim and proj.shape[1] == 2 * qk_dim + 2 * v_dim

    pos = jnp.arange(seq, dtype=F32)
    inv_freq = 1.0 / (ROPE_BASE ** jnp.linspace(0.0, 1.0, dk // 2, dtype=F32))
    theta = pos[:, None] * inv_freq[None, :]
    cos, sin = jnp.cos(theta), jnp.sin(theta)
    log_gamma = jnp.log1p(-jnp.exp2(-5.0 - jnp.arange(heads, dtype=F32)))
    idx = jnp.arange(CHUNK, dtype=F32)
    causal = jnp.tril(jnp.ones((CHUNK, CHUNK), dtype=bool))
    idec = jnp.exp(jnp.where(causal[None],
                             (idx[:, None] - idx[None, :])[None] * log_gamma[:, None, None],
                             -jnp.inf))
    qdec = jnp.exp(log_gamma[:, None] * (idx + 1.0)[None, :])[..., None]
    kdec = jnp.exp(log_gamma[:, None] * (CHUNK - 1.0 - idx)[None, :])[..., None]
    cdec = jnp.exp(CHUNK * log_gamma)[:, None, None]

    full = lambda shape: pl.BlockSpec(shape, lambda c: (0,) * len(shape))
    proj3 = proj.reshape(batch, seq, proj.shape[1])
    out = pl.pallas_call(
        functools.partial(_ret_kernel, batch=batch, heads=heads, dk=dk, dv=dv),
        grid=(nc,),
        in_specs=[pl.BlockSpec((batch, CHUNK, qk_dim), lambda c: (0, c, 0)),
                  pl.BlockSpec((batch, CHUNK, qk_dim), lambda c: (0, c, 1)),
                  pl.BlockSpec((batch, CHUNK, v_dim), lambda c: (0, c, 1)),
                  pl.BlockSpec((batch, CHUNK, v_dim), lambda c: (0, c, 2)),
                  pl.BlockSpec((CHUNK, dk // 2), lambda c: (c, 0)),
                  pl.BlockSpec((CHUNK, dk // 2), lambda c: (c, 0)),
                  full((heads, CHUNK, CHUNK)), full((heads, CHUNK, 1)),
                  full((heads, CHUNK, 1)), full((heads, 1, 1))],
        out_specs=pl.BlockSpec((batch, CHUNK, v_dim), lambda c: (0, c, 0)),
        out_shape=jax.ShapeDtypeStruct((batch, seq, v_dim), BF16),
        scratch_shapes=[pltpu.VMEM((batch, heads, dk, dv), F32)],
        compiler_params=_params(("arbitrary",)),
        name=name,
    )(proj3, proj3, proj3, proj3, cos, sin, idec, qdec, kdec, cdec)
    return out.reshape(t, v_dim)


def _deinterleave_kernel(w_ref, perm_ref, o_ref, *, n_perm, head_dim):
    @pl.when(pl.program_id(0) < n_perm)
    def _():
        for lo in range(0, o_ref.shape[1], head_dim):
            cols = slice(lo, lo + head_dim)
            o_ref[:, cols] = jnp.dot(w_ref[:, cols].astype(BF16), perm_ref[...],
                                     preferred_element_type=F32).astype(o_ref.dtype)

    @pl.when(pl.program_id(0) >= n_perm)
    def _():
        o_ref[...] = w_ref[...].astype(o_ref.dtype)


def deinterleave_qk_columns(w, layer, *, qk_cols, head_dim, tn, name):
    k, n = w.shape[1], w.shape[2]
    assert qk_cols % tn == 0 and tn % head_dim == 0
    src = np.arange(head_dim).reshape(head_dim // 2, 2).T.reshape(-1)
    perm = jnp.asarray(np.eye(head_dim, dtype=np.float32)[:, src], dtype=BF16)
    return pl.pallas_call(
        functools.partial(_deinterleave_kernel, n_perm=qk_cols // tn, head_dim=head_dim),
        grid=(n // tn,),
        in_specs=[pl.BlockSpec((None, k, tn), lambda j: (layer, 0, j)),
                  pl.BlockSpec((head_dim, head_dim), lambda j: (0, 0))],
        out_specs=pl.BlockSpec((k, tn), lambda j: (0, j)),
        out_shape=jax.ShapeDtypeStruct((k, n), BF16),
        compiler_params=_params(("arbitrary",)),
        name=name,
    )(w, perm)


def _cast_kernel(w_ref, o_ref):
    o_ref[...] = w_ref[...].astype(o_ref.dtype)


def cast_columns_bf16(w, *, col_start, n_cols, tn, name):
    layers, k, _ = w.shape
    assert col_start % tn == 0 and n_cols % tn == 0
    first = col_start // tn
    return pl.pallas_call(
        _cast_kernel,
        grid=(layers, n_cols // tn),
        in_specs=[pl.BlockSpec((None, k, tn), lambda l, j: (l, 0, j + first))],
        out_specs=pl.BlockSpec((None, k, tn), lambda l, j: (l, 0, j)),
        out_shape=jax.ShapeDtypeStruct((layers, k, n_cols), BF16),
        compiler_params=_params(("arbitrary", "arbitrary")),
        name=name,
    )(w)


def kernel(x, mem, ssd_in_w, ssd_conv_w, ssd_conv_b, ssd_dt_bias, ssd_a_log, ssd_d, ssd_norm_w,
           ssd_out_w, ret_in_w, ret_out_w, mixer_norm_w, mem_q_norm_w, mem_kv_norm_w, mem_q_w,
           mem_kv_w, mem_o_w, ffn_norm_w, ffn_in_w, ffn_out_w, final_norm_w):
    batch, seq, d = x.shape
    depth = mixer_norm_w.shape[0]
    xf = x.reshape(batch * seq, d)
    memf = mem.reshape(-1, d)
    ssd_inner = ssd_norm_w.shape[1]
    ssd_main = 2 * ssd_inner + 2 * SSD_GROUPS * SSD_STATE
    ssd_heads = ssd_inner // SSD_HEAD_DIM
    ret_qk = ret_in_w.shape[2] // 6
    ret_v = 2 * ret_qk

    ssd_in_b, ssd_out_b = ssd_in_w.astype(BF16), ssd_out_w.astype(BF16)
    ssd_dt_b = jnp.pad(ssd_in_b[:, :, ssd_main:], ((0, 0), (0, 0), (0, LANES - ssd_heads)))
    ret_out_b = ret_out_w.astype(BF16)
    mem_q_b, mem_o_b = mem_q_w.astype(BF16), mem_o_w.astype(BF16)
    ffn_hidden = ffn_out_w.shape[1]
    ffn_gate_b = cast_columns_bf16(ffn_in_w, col_start=0, n_cols=ffn_hidden, tn=512,
                                   name="cast_ffn_gate")
    ffn_up_b = cast_columns_bf16(ffn_in_w, col_start=ffn_hidden, n_cols=ffn_hidden, tn=512,
                                 name="cast_ffn_up")
    ffn_out_b = ffn_out_w.astype(BF16)

    for i in range(depth):
        j = i // 2
        if i % 2 == 0:
            w_dt = ssd_dt_b[j]
            zx, dt_raw = norm_matmul_extra(xf, mixer_norm_w[i], ssd_in_b, j, w_dt, n_out=ssd_main,
                                           tm=1024, tn=2048, out_dtype=BF16, name=f"ssd_in_{i}")
            y = ssd_scan(zx, dt_raw, ssd_conv_w[j], ssd_conv_b[j], ssd_dt_bias[j], ssd_a_log[j],
                         ssd_d[j], ssd_norm_w[j], batch=batch, name=f"ssd_scan_{i}")
            xf = matmul_residual(y, ssd_out_b, j, xf, tm=1024, tn=1024, tk=y.shape[1],
                                 name=f"ssd_out_{i}")
        else:
            w_in = deinterleave_qk_columns(ret_in_w, j, qk_cols=2 * ret_qk,
                                           head_dim=ret_qk // RET_HEADS, tn=1024,
                                           name=f"ret_perm_{i}")
            proj = norm_matmul(xf, mixer_norm_w[i], w_in[None], 0, n_out=w_in.shape[1], tm=1024,
                               tn=2048, out_dtype=BF16, name=f"ret_in_{i}")
            y = retention_scan(proj, batch=batch, qk_dim=ret_qk, v_dim=ret_v,
                               name=f"ret_scan_{i}")
            xf = matmul_residual(y, ret_out_b, j, xf, tm=1024, tn=1024, tk=y.shape[1],
                                 name=f"ret_out_{i}")

        kv = norm_matmul(memf, mem_kv_norm_w[i], mem_kv_w, i, n_out=2 * d, tm=512, tn=1024,
                         out_dtype=BF16, name=f"mem_kv_{i}")
        xf = memory_block(xf, mem_q_norm_w[i], mem_q_b, mem_o_b, i, kv, batch=batch,
                          heads=MEM_HEADS, tm=512, name=f"mem_blk_{i}")

        hid = norm_glu(xf, ffn_norm_w[i], ffn_gate_b, ffn_up_b, i, tm=1024, tn=512,
                       name=f"ffn_in_{i}")
        if i < depth - 1:
            xf = matmul_residual(hid, ffn_out_b, i, xf, tm=1024, tn=1024,
                                 tk=hid.shape[1] // 2, name=f"ffn_out_{i}")
        else:
            xf = matmul_residual_norm(hid, ffn_out_b, i, xf, final_norm_w, tm=512,
                                      tk=hid.shape[1] // 2, name=f"ffn_out_{i}")

    return xf.reshape(batch, seq, d)
```
